```python
import math
import jax, jax.numpy as jnp
from jax import lax
import numpy as np

D_MODEL = 1024
BATCH = 8
SEQ = 4096
DEPTH = 1

HEAD_DIM = 64
N_HEADS = D_MODEL // HEAD_DIM
H_FOX = N_HEADS // 2
H_MOBA = N_HEADS - H_FOX
W_FOX = H_FOX * HEAD_DIM
W_MOBA = H_MOBA * HEAD_DIM
IN_COLS = 3 * W_FOX + H_FOX + 3 * W_MOBA
Q_BLOCK = 128
MOBA_BLOCK = 256
MOBA_TOPK = 3
N_BUCKETS = 32
MAX_DISTANCE = 128
D_FF = -(-8 * D_MODEL // (3 * 256)) * 256
EPS = 1e-6
FORGET_BIAS_INIT = 3.0

kernel_name = "hybrid_fox_moba_adaln_block"


def rms_norm(x, g):
    xf = x.astype(jnp.float32)
    y = xf * lax.rsqrt(jnp.mean(xf * xf, axis=-1, keepdims=True) + EPS)
    return (y * g.astype(jnp.float32)).astype(x.dtype)


def t5_bucket(dist):
    n = jnp.maximum(dist, 0)
    max_exact = N_BUCKETS // 2
    nf = jnp.maximum(n, 1).astype(jnp.float32)
    large = max_exact + (jnp.log(nf / max_exact) / math.log(MAX_DISTANCE / max_exact)
                         * (N_BUCKETS - max_exact)).astype(jnp.int32)
    large = jnp.minimum(large, N_BUCKETS - 1)
    return jnp.where(n < max_exact, n, large)


def fox_attention(q, k, v, log_f_cum):
    B, H, S, Dh = q.shape
    n_blocks = S // Q_BLOCK
    scale = HEAD_DIM ** -0.5
    k_pos = jnp.arange(S)

    def block(i):
        start = i * Q_BLOCK
        q_blk = lax.dynamic_slice_in_dim(q, start, Q_BLOCK, axis=2)
        f_blk = lax.dynamic_slice_in_dim(log_f_cum, start, Q_BLOCK, axis=2)
        s = jnp.einsum('bhqd,bhkd->bhqk', q_blk, k,
                       preferred_element_type=jnp.float32) * scale
        s = s + f_blk[..., :, None] - log_f_cum[..., None, :]
        q_pos = start + jnp.arange(Q_BLOCK)
        causal = k_pos[None, :] <= q_pos[:, None]
        s = jnp.where(causal, s, -jnp.inf)
        p = jax.nn.softmax(s, axis=-1)
        return jnp.einsum('bhqk,bhkd->bhqd', p.astype(v.dtype), v)

    out = lax.map(block, jnp.arange(n_blocks))
    return jnp.transpose(out, (1, 0, 3, 2, 4)).reshape(B, S, H * Dh)


def moba_attention(q, k, v, rel_bias):
    B, H, S, Dh = q.shape
    nb = -(-S // MOBA_BLOCK)
    pad = nb * MOBA_BLOCK - S
    kb = jnp.pad(k, ((0, 0), (0, 0), (0, pad), (0, 0))).reshape(B, H, nb, MOBA_BLOCK, Dh)
    vb = jnp.pad(v, ((0, 0), (0, 0), (0, pad), (0, 0))).reshape(B, H, nb, MOBA_BLOCK, Dh)
    counts = jnp.clip(S - jnp.arange(nb) * MOBA_BLOCK, 1, MOBA_BLOCK).astype(jnp.float32)
    k_mean = jnp.sum(kb.astype(jnp.float32), axis=3) / counts[None, None, :, None]
    n_q = S // Q_BLOCK
    k_sel = min(MOBA_TOPK, nb)
    scale = HEAD_DIM ** -0.5
    offs = jnp.arange(MOBA_BLOCK)
    blk_ids = jnp.arange(nb)
    h_ix3 = jnp.arange(H)[:, None, None]
    h_ix4 = jnp.arange(H)[:, None, None, None]
    bias_hb = rel_bias.T.astype(jnp.float32)

    def chunk(n):
        b = n // n_q
        i = n % n_q
        start = i * Q_BLOCK
        q_b = lax.dynamic_index_in_dim(q, b, axis=0, keepdims=False)
        kb_b = lax.dynamic_index_in_dim(kb, b, axis=0, keepdims=False)
        vb_b = lax.dynamic_index_in_dim(vb, b, axis=0, keepdims=False)
        km_b = lax.dynamic_index_in_dim(k_mean, b, axis=0, keepdims=False)
        q_c = lax.dynamic_slice_in_dim(q_b, start, Q_BLOCK, axis=1)
        q_pos = start + jnp.arange(Q_BLOCK)
        own = start // MOBA_BLOCK
        g = jnp.einsum('hqd,hnd->hqn', q_c.astype(jnp.float32), km_b)
        g = jnp.where(blk_ids[None, None, :] < own, g, -jnp.inf)
        _, idx = lax.top_k(g, k_sel)
        valid = idx < own
        k_g = kb_b[h_ix3, idx]
        v_g = vb_b[h_ix3, idx]
        s_sel = jnp.einsum('hqd,hqnkd->hqnk', q_c, k_g,
                           preferred_element_type=jnp.float32) * scale
        pos_sel = idx[..., None] * MOBA_BLOCK + offs
        s_sel = s_sel + bias_hb[h_ix4, t5_bucket(q_pos[None, :, None, None] - pos_sel)]
        s_sel = jnp.where(valid[..., None], s_sel, -jnp.inf)
        k_own = lax.dynamic_index_in_dim(kb_b, own, axis=1, keepdims=False)
        v_own = lax.dynamic_index_in_dim(vb_b, own, axis=1, keepdims=False)
        s_own = jnp.einsum('hqd,hkd->hqk', q_c, k_own,
                           preferred_element_type=jnp.float32) * scale
        pos_own = own * MOBA_BLOCK + offs
        bkt_own = t5_bucket(q_pos[:, None] - pos_own[None, :])
        s_own = s_own + jnp.transpose(rel_bias.astype(jnp.float32)[bkt_own], (2, 0, 1))
        s_own = jnp.where(pos_own[None, None, :] <= q_pos[None, :, None], s_own, -jnp.inf)
        s_all = jnp.concatenate([s_sel.reshape(H, Q_BLOCK, k_sel * MOBA_BLOCK), s_own], axis=-1)
        p = jax.nn.softmax(s_all, axis=-1).astype(v.dtype)
        p_sel = p[..., :k_sel * MOBA_BLOCK].reshape(H, Q_BLOCK, k_sel, MOBA_BLOCK)
        p_own = p[..., k_sel * MOBA_BLOCK:]
        return (jnp.einsum('hqnk,hqnkd->hqd', p_sel, v_g)
                + jnp.einsum('hqk,hkd->hqd', p_own, v_own))

    out = lax.map(chunk, jnp.arange(B * n_q))
    out = out.reshape(B, n_q, H, Q_BLOCK, Dh)
    return jnp.transpose(out, (0, 1, 3, 2, 4)).reshape(B, S, H * Dh)


def setup_inputs(seed: int = 0) -> dict:
    key = jax.random.key(seed)
    ks = jax.random.split(key, 20)
    f32 = jnp.float32
    nrm = lambda k, shape: jax.random.normal(k, shape, dtype=f32)
    return {
        "x": nrm(ks[0], (BATCH, SEQ, D_MODEL)),
        "c": nrm(ks[1], (BATCH, D_MODEL)),
        "w_ada": nrm(ks[2], (DEPTH, D_MODEL, 6 * D_MODEL)) * (0.5 * D_MODEL ** -0.5),
        "b_ada": nrm(ks[3], (DEPTH, 6 * D_MODEL)) * 0.02,
        "norm1": 1.0 + 0.02 * nrm(ks[4], (DEPTH, D_MODEL)),
        "norm2": 1.0 + 0.02 * nrm(ks[5], (DEPTH, D_MODEL)),
        "w_in": nrm(ks[6], (DEPTH, D_MODEL, IN_COLS)) * D_MODEL ** -0.5,
        "b_forget": FORGET_BIAS_INIT + 0.5 * nrm(ks[7], (DEPTH, H_FOX)),
        "q_norm_fox": 1.0 + 0.02 * nrm(ks[8], (DEPTH, HEAD_DIM)),
        "k_norm_fox": 1.0 + 0.02 * nrm(ks[9], (DEPTH, HEAD_DIM)),
        "q_norm_moba": 1.0 + 0.02 * nrm(ks[10], (DEPTH, HEAD_DIM)),
        "k_norm_moba": 1.0 + 0.02 * nrm(ks[11], (DEPTH, HEAD_DIM)),
        "rel_bias": 0.2 * nrm(ks[12], (N_BUCKETS, H_MOBA)),
        "w_o": nrm(ks[13], (DEPTH, D_MODEL, D_MODEL)) * D_MODEL ** -0.5,
        "w_gate": nrm(ks[14], (DEPTH, D_MODEL, D_FF)) * D_MODEL ** -0.5,
        "w_up": nrm(ks[15], (DEPTH, D_MODEL, D_FF)) * D_MODEL ** -0.5,
        "w_down": nrm(ks[16], (DEPTH, D_FF, D_MODEL)) * D_FF ** -0.5,
    }


def reference(x, c, w_ada, b_ada, norm1, norm2, w_in, b_forget, q_norm_fox, k_norm_fox,
              q_norm_moba, k_norm_moba, rel_bias, w_o, w_gate, w_up, w_down):
    B, S, D = x.shape

    def heads(t, h):
        return jnp.transpose(t.reshape(B, S, h, HEAD_DIM), (0, 2, 1, 3))

    for l in range(DEPTH):
        mod = (jax.nn.silu(c) @ w_ada[l] + b_ada[l]).reshape(B, 6, D)[:, :, None, :]
        shift1, scale1, gate1, shift2, scale2, gate2 = [mod[:, j] for j in range(6)]

        h = rms_norm(x, norm1[l]) * (1 + scale1) + shift1
        proj = h @ w_in[l]
        o0 = 0
        fq = proj[..., o0:o0 + W_FOX]; o0 += W_FOX
        fk = proj[..., o0:o0 + W_FOX]; o0 += W_FOX
        fv = proj[..., o0:o0 + W_FOX]; o0 += W_FOX
        ff = proj[..., o0:o0 + H_FOX]; o0 += H_FOX
        mq = proj[..., o0:o0 + W_MOBA]; o0 += W_MOBA
        mk = proj[..., o0:o0 + W_MOBA]; o0 += W_MOBA
        mv = proj[..., o0:o0 + W_MOBA]

        fq = rms_norm(heads(fq, H_FOX), q_norm_fox[l])
        fk = rms_norm(heads(fk, H_FOX), k_norm_fox[l])
        fv = heads(fv, H_FOX)
        log_f = jax.nn.log_sigmoid(ff.astype(jnp.float32) + b_forget[l].astype(jnp.float32))
        log_f_cum = jnp.transpose(jnp.cumsum(log_f, axis=1), (0, 2, 1))
        fox_out = fox_attention(fq, fk, fv, log_f_cum)

        mq = rms_norm(heads(mq, H_MOBA), q_norm_moba[l])
        mk = rms_norm(heads(mk, H_MOBA), k_norm_moba[l])
        mv = heads(mv, H_MOBA)
        moba_out = moba_attention(mq, mk, mv, rel_bias)

        mix = jnp.concatenate([fox_out, moba_out], axis=-1).astype(x.dtype) @ w_o[l]
        x = x + gate1 * mix

        h2 = rms_norm(x, norm2[l]) * (1 + scale2) + shift2
        ffn = (jax.nn.silu(h2 @ w_gate[l]) * (h2 @ w_up[l])) @ w_down[l]
        x = x + gate2 * ffn
    return x
```

```python
import functools
import math

import numpy as np
import jax
import jax.numpy as jnp
from jax import lax
from jax.experimental import pallas as pl
from jax.experimental.pallas import tpu as pltpu

F32 = jnp.float32
BF16 = jnp.bfloat16

EPS = 1e-6
HEAD_DIM = 64
PAIR = 2 * HEAD_DIM
KV_BLOCK = 256
MOBA_TOPK = 3
MAX_DISTANCE = 128
ROW_TILE = 256
VMEM_LIMIT = 56 * 1024 * 1024
NEG_INF = float("-inf")


def _split3(x):
    hi = x.astype(BF16)
    r = x - hi.astype(F32)
    mid = r.astype(BF16)
    lo = (r - mid.astype(F32)).astype(BF16)
    return hi, mid, lo


def _dot(a, b):
    return jnp.dot(a, b, preferred_element_type=F32)


def _dot_nt(a, b):
    return lax.dot_general(a, b, (((1,), (1,)), ((), ())), preferred_element_type=F32)


def _params(sem):
    return pltpu.CompilerParams(dimension_semantics=sem, vmem_limit_bytes=VMEM_LIMIT)


def _const_spec(shape):
    nd = len(shape)
    return pl.BlockSpec(shape, lambda *_: (0,) * nd, pipeline_mode=pl.Buffered(1))


def _adaln_kernel(c_ref, w_ref, b_ref, o_ref):
    c = c_ref[...]
    a = c * jax.nn.sigmoid(c)
    w = w_ref[...]
    a_hi, a_mid, a_lo = _split3(a)
    w_hi, w_mid, w_lo = _split3(w)
    acc = _dot(a_lo, w_hi) + _dot(a_mid, w_mid) + _dot(a_hi, w_lo)
    acc = acc + _dot(a_mid, w_hi) + _dot(a_hi, w_mid)
    acc = acc + _dot(a_hi, w_hi)
    o_ref[...] = acc + b_ref[...]


def _adaln(c, w, b):
    bsz, d = c.shape
    n = w.shape[1]
    tn = 1024 if n % 1024 == 0 else n
    return pl.pallas_call(
        _adaln_kernel,
        grid=(n // tn,),
        in_specs=[
            pl.BlockSpec((bsz, d), lambda i: (0, 0)),
            pl.BlockSpec((d, tn), lambda i: (0, i)),
            pl.BlockSpec((1, tn), lambda i: (0, i)),
        ],
        out_specs=pl.BlockSpec((bsz, tn), lambda i: (0, i)),
        out_shape=jax.ShapeDtypeStruct((bsz, n), F32),
        compiler_params=_params(("parallel",)),
        name="adaln",
    )(c, w, b.reshape(1, n))


def _head_rms(y, ind, gain):
    cols = []
    for c0 in range(0, y.shape[1], 256):
        yc = y[:, c0:c0 + 256]
        sq = yc * yc
        sq_hi = sq.astype(BF16)
        sq_lo = (sq - sq_hi.astype(F32)).astype(BF16)
        ss = _dot(sq_hi, ind) + _dot(sq_lo, ind)
        cols.append(yc * lax.rsqrt(ss * (1.0 / HEAD_DIM) + EPS) * gain[:, c0:c0 + 256])
    return jnp.concatenate(cols, axis=1)


def _in_proj_kernel(x_ref, shift_ref, scale_ref, g_ref, w_ref, ind_ref, gains_ref,
                    fq_ref, fk_ref, fv_ref, mq_ref, mk_ref, mv_ref, ff_ref, kmean_ref, *, wf, wm):
    t = pl.program_id(1)
    x = x_ref[0]
    ms = jnp.mean(x * x, axis=-1, keepdims=True)
    y = x * lax.rsqrt(ms + EPS) * g_ref[...]
    h = y * (1.0 + scale_ref[0]) + shift_ref[0]
    proj = _dot(h.astype(BF16), w_ref[...])
    ind = ind_ref[...]
    gains = gains_ref[...]
    o = 0
    fq = _head_rms(proj[:, o:o + wf], ind, gains[0:1, :wf]); o += wf
    fk = _head_rms(proj[:, o:o + wf], ind, gains[1:2, :wf]); o += wf
    fv = proj[:, o:o + wf]; o += wf
    mq = _head_rms(proj[:, o:o + wm], ind, gains[2:3, :wm]); o += wm
    mk = _head_rms(proj[:, o:o + wm], ind, gains[3:4, :wm]); o += wm
    mv = proj[:, o:o + wm]; o += wm
    ff = proj[:, o:o + ff_ref.shape[2]]
    fq_ref[0] = fq.astype(BF16)
    fk_ref[0] = fk.astype(BF16)
    fv_ref[0] = fv.astype(BF16)
    mq_ref[0] = mq.astype(BF16)
    mk_ref[0] = mk.astype(BF16)
    mv_ref[0] = mv.astype(BF16)
    ff_ref[0] = ff
    kmean_ref[0, pl.ds(t, 1), :] = jnp.sum(mk, axis=0, keepdims=True) * (1.0 / KV_BLOCK)


def _in_proj(x, shift, scale, g, w_all, ind, gains, wf, wm, hf):
    bsz, s, d = x.shape
    tm = ROW_TILE
    nb = s // KV_BLOCK
    ncol = w_all.shape[1]
    row = lambda b, t: (b, t, 0)
    per_b = lambda b, t: (b, 0, 0)
    outs = pl.pallas_call(
        functools.partial(_in_proj_kernel, wf=wf, wm=wm),
        grid=(bsz, s // tm),
        in_specs=[
            pl.BlockSpec((1, tm, d), row),
            pl.BlockSpec((1, 1, d), per_b),
            pl.BlockSpec((1, 1, d), per_b),
            _const_spec((1, d)),
            _const_spec((d, ncol)),
            _const_spec((256, 256)),
            _const_spec(gains.shape),
        ],
        out_specs=[
            pl.BlockSpec((1, tm, wf), row),
            pl.BlockSpec((1, tm, wf), row),
            pl.BlockSpec((1, tm, wf), row),
            pl.BlockSpec((1, tm, wm), row),
            pl.BlockSpec((1, tm, wm), row),
            pl.BlockSpec((1, tm, wm), row),
            pl.BlockSpec((1, tm, hf), row),
            pl.BlockSpec((1, nb, wm), per_b),
        ],
        out_shape=[
            jax.ShapeDtypeStruct((bsz, s, wf), BF16),
            jax.ShapeDtypeStruct((bsz, s, wf), BF16),
            jax.ShapeDtypeStruct((bsz, s, wf), BF16),
            jax.ShapeDtypeStruct((bsz, s, wm), BF16),
            jax.ShapeDtypeStruct((bsz, s, wm), BF16),
            jax.ShapeDtypeStruct((bsz, s, wm), BF16),
            jax.ShapeDtypeStruct((bsz, s, hf), F32),
            jax.ShapeDtypeStruct((bsz, nb, wm), F32),
        ],
        compiler_params=_params(("parallel", "arbitrary")),
        name="in_proj",
    )(x, shift, scale, g, w_all, ind, gains)
    return outs


def _forget_kernel(ff_ref, b_ref, tri_ref, blk_ref, o_ref):
    z = ff_ref[0] + b_ref[...]
    lf = jnp.minimum(z, 0.0) - jnp.log1p(jnp.exp(-jnp.abs(z)))
    tri = tri_ref[...]
    hi, mid, lo = _split3(lf)
    local = _dot(lo, tri) + _dot(mid, tri) + _dot(hi, tri)
    tot = jnp.broadcast_to(local[:, KV_BLOCK - 1:KV_BLOCK], (local.shape[0], 128))
    blk = blk_ref[...]
    t_hi, t_mid, t_lo = _split3(tot)
    off = _dot(blk, t_lo) + _dot(blk, t_mid) + _dot(blk, t_hi)
    o_ref[0] = local + off[:, 0:1]


def _forget(ff_rows, b_rows, tri, blk):
    bsz, rows, _ = ff_rows.shape
    return pl.pallas_call(
        _forget_kernel,
        grid=(bsz,),
        in_specs=[
            pl.BlockSpec((1, rows, KV_BLOCK), lambda b: (b, 0, 0)),
            pl.BlockSpec((rows, 1), lambda b: (0, 0)),
            pl.BlockSpec((KV_BLOCK, KV_BLOCK), lambda b: (0, 0)),
            pl.BlockSpec((rows, rows), lambda b: (0, 0)),
        ],
        out_specs=pl.BlockSpec((1, rows, KV_BLOCK), lambda b: (b, 0, 0)),
        out_shape=jax.ShapeDtypeStruct((bsz, rows, KV_BLOCK), F32),
        compiler_params=_params(("parallel",)),
        name="forget",
    )(ff_rows, b_rows, tri, blk)


def _t5_bucket_np(dist, n_buckets):
    n = np.maximum(dist, 0)
    max_exact = n_buckets // 2
    nf = np.maximum(n, 1).astype(np.float32)
    large = max_exact + (np.log(nf / np.float32(max_exact)) / np.float32(math.log(MAX_DISTANCE / max_exact))
                         * np.float32(n_buckets - max_exact)).astype(np.int32)
    large = np.minimum(large, n_buckets - 1)
    return np.where(n < max_exact, n, large).astype(np.int32)


def _t5_kernel(relb_ref, bkt_ref, o_ref, *, n_buckets):
    h = pl.program_id(0)
    kl = lax.broadcasted_iota(jnp.int32, (KV_BLOCK, KV_BLOCK), 0)
    ql = lax.broadcasted_iota(jnp.int32, (KV_BLOCK, KV_BLOCK), 1)
    for t in range(2):
        b = bkt_ref[t]
        acc = jnp.zeros((KV_BLOCK, KV_BLOCK), F32)
        for u in range(n_buckets):
            acc = jnp.where(b == u, relb_ref[h, u], acc)
        if t == 0:
            acc = jnp.where(kl <= ql, acc, NEG_INF)
        o_ref[0, t] = acc


def _t5_tables(rel_bias_t, bkt):
    nh, n_buckets = rel_bias_t.shape
    return pl.pallas_call(
        functools.partial(_t5_kernel, n_buckets=n_buckets),
        grid=(nh,),
        in_specs=[
            pl.BlockSpec(memory_space=pltpu.SMEM),
            pl.BlockSpec((2, KV_BLOCK, KV_BLOCK), lambda h: (0, 0, 0)),
        ],
        out_specs=pl.BlockSpec((1, 2, KV_BLOCK, KV_BLOCK), lambda h: (h, 0, 0, 0)),
        out_shape=jax.ShapeDtypeStruct((nh, 2, KV_BLOCK, KV_BLOCK), F32),
        compiler_params=_params(("parallel",)),
        name="t5_tables",
    )(rel_bias_t, bkt)


def _fox_kernel(q_ref, k_ref, v_ref, fcol_ref, frow_ref, o_ref):
    i = pl.program_id(2)
    tq = q_ref.shape[1]
    tk = KV_BLOCK
    qf = q_ref[0].astype(F32) * (HEAD_DIM ** -0.5)
    lane = lax.broadcasted_iota(jnp.int32, (tq, PAIR), 1)
    fcol = fcol_ref[0, 0]
    row = lax.broadcasted_iota(jnp.int32, (tq, tk), 0)
    col = lax.broadcasted_iota(jnp.int32, (tq, tk), 1)
    outs = []
    for hh in range(2):
        in_head = (lane >= HEAD_DIM) if hh else (lane < HEAD_DIM)
        qh = jnp.where(in_head, qf, 0.0).astype(BF16)
        fq = fcol[:, hh:hh + 1]

        def step(kb, carry, masked):
            m, l, acc = carry
            start = pl.multiple_of(kb * tk, tk)
            k = k_ref[0, pl.ds(start, tk), :]
            v = v_ref[0, pl.ds(start, tk), :]
            fk = frow_ref[0, 0, kb][hh:hh + 1, :]
            s = _dot_nt(qh, k) + fq - fk
            if masked:
                s = jnp.where(col <= row, s, NEG_INF)
            m_new = jnp.maximum(m, jnp.max(s, axis=1, keepdims=True))
            alpha = jnp.exp(m - m_new)
            p = jnp.exp(s - m_new)
            l = alpha * l + jnp.sum(p, axis=1, keepdims=True)
            acc = alpha * acc + _dot(p.astype(BF16), v)
            return m_new, l, acc

        init = (jnp.full((tq, 1), NEG_INF, F32), jnp.zeros((tq, 1), F32), jnp.zeros((tq, PAIR), F32))
        carry = lax.fori_loop(0, i, functools.partial(step, masked=False), init)
        m, l, acc = step(i, carry, True)
        outs.append(acc / l)
    o_ref[0] = jnp.where(lane < HEAD_DIM, outs[0], outs[1]).astype(o_ref.dtype)


def _fox(q, k, v, fcol, frow):
    bsz, s, w = q.shape
    npair = w // PAIR
    nb = s // KV_BLOCK
    tq = KV_BLOCK
    return pl.pallas_call(
        _fox_kernel,
        grid=(bsz, npair, s // tq),
        in_specs=[
            pl.BlockSpec((1, tq, PAIR), lambda b, j, i: (b, i, j)),
            pl.BlockSpec((1, s, PAIR), lambda b, j, i: (b, 0, j)),
            pl.BlockSpec((1, s, PAIR), lambda b, j, i: (b, 0, j)),
            pl.BlockSpec((1, 1, tq, 2), lambda b, j, i: (b, j, i, 0)),
            pl.BlockSpec((1, 1, nb, 2, KV_BLOCK), lambda b, j, i: (b, j, 0, 0, 0)),
        ],
        out_specs=pl.BlockSpec((1, tq, PAIR), lambda b, j, i: (b, i, j)),
        out_shape=jax.ShapeDtypeStruct((bsz, s, w), BF16),
        compiler_params=_params(("parallel", "parallel", "arbitrary")),
        name="fox",
    )(q, k, v, fcol, frow)


def _moba_kernel(far_ref, q_ref, k_ref, vt_ref, kmean_ref, tbl_ref, o_ref, sel_ref):
    j = pl.program_id(1)
    i = pl.program_id(2)
    tq = q_ref.shape[1]
    tk = KV_BLOCK
    nb = kmean_ref.shape[1]
    qf = q_ref[0].astype(F32)
    lane = lax.broadcasted_iota(jnp.int32, (tq, PAIR), 1)
    klane = lax.broadcasted_iota(jnp.int32, (nb, PAIR), 1)
    blk = lax.broadcasted_iota(jnp.int32, (nb, tq), 0)
    orow = lax.broadcasted_iota(jnp.int32, (PAIR, tq), 0)
    kmean = kmean_ref[0]
    outs = []
    for hh in range(2):
        in_head = (lane >= HEAD_DIM) if hh else (lane < HEAD_DIM)
        q_gate = jnp.where(in_head, qf, 0.0).astype(BF16)
        qh = jnp.where(in_head, qf * (HEAD_DIM ** -0.5), 0.0).astype(BF16)
        far = far_ref[2 * j + hh]

        km = jnp.where((klane >= HEAD_DIM) if hh else (klane < HEAD_DIM), kmean, 0.0)
        km_hi, km_mid, km_lo = _split3(km)
        g = _dot_nt(km_lo, q_gate) + _dot_nt(km_mid, q_gate) + _dot_nt(km_hi, q_gate)
        valid = blk < i
        g = jnp.where(valid, g, NEG_INF)
        rank = jnp.zeros((nb, tq), jnp.int32)
        for mm in range(nb):
            gm = g[mm:mm + 1, :]
            beats = (gm > g) | ((gm == g) & (blk > mm))
            rank = rank + beats.astype(jnp.int32)
        sel = valid & (rank < MOBA_TOPK)
        sel_ref[...] = jnp.where(sel, 0.0, NEG_INF)

        def block(kb, carry, bias):
            m, l, acc = carry
            start = pl.multiple_of(kb * tk, tk)
            k = k_ref[0, pl.ds(start, tk), :]
            st = _dot_nt(k, qh) + bias
            m_new = jnp.maximum(m, jnp.max(st, axis=0, keepdims=True))
            alpha = jnp.exp(m - m_new)
            p = jnp.exp(st - m_new)
            l = alpha * l + jnp.sum(p, axis=0, keepdims=True)
            acc = alpha * acc + _dot(vt_ref[0, 0, kb], p.astype(BF16))
            return m_new, l, acc

        init = (jnp.full((1, tq), NEG_INF, F32), jnp.zeros((1, tq), F32), jnp.zeros((PAIR, tq), F32))
        carry = block(i, init, tbl_ref[hh, 0])
        prev = jnp.maximum(i - 1, 0)
        carry = block(prev, carry, tbl_ref[hh, 1] + sel_ref[pl.ds(prev, 1), :])

        def far_block(kb, carry):
            return block(kb, carry, sel_ref[pl.ds(kb, 1), :] + far)

        m, l, acc = lax.fori_loop(0, jnp.maximum(i - 1, 0), far_block, carry)
        outs.append(acc / l)
    out_t = jnp.where(orow < HEAD_DIM, outs[0], outs[1])
    o_ref[0] = out_t.T.astype(o_ref.dtype)


def _moba(far, q, k, vt, kmean, tbl):
    bsz, s, w = q.shape
    npair = w // PAIR
    nb = s // KV_BLOCK
    tq = KV_BLOCK
    return pl.pallas_call(
        _moba_kernel,
        grid=(bsz, npair, s // tq),
        in_specs=[
            pl.BlockSpec(memory_space=pltpu.SMEM),
            pl.BlockSpec((1, tq, PAIR), lambda b, j, i: (b, i, j)),
            pl.BlockSpec((1, s, PAIR), lambda b, j, i: (b, 0, j)),
            pl.BlockSpec((1, 1, nb, PAIR, KV_BLOCK), lambda b, j, i: (b, j, 0, 0, 0)),
            pl.BlockSpec((1, nb, PAIR), lambda b, j, i: (b, 0, j)),
            pl.BlockSpec((2, 2, KV_BLOCK, KV_BLOCK), lambda b, j, i: (j, 0, 0, 0)),
        ],
        out_specs=pl.BlockSpec((1, tq, PAIR), lambda b, j, i: (b, i, j)),
        out_shape=jax.ShapeDtypeStruct((bsz, s, w), BF16),
        scratch_shapes=[pltpu.VMEM((nb, tq), F32)],
        compiler_params=_params(("parallel", "parallel", "arbitrary")),
        name="moba",
    )(far, q, k, vt, kmean, tbl)


def _out_proj_kernel(fox_ref, moba_ref, x_ref, wof_ref, wom_ref, gate_ref, shift_ref, scale_ref, g_ref,
                     x1_ref, h2_ref):
    mix = _dot(fox_ref[0], wof_ref[...]) + _dot(moba_ref[0], wom_ref[...])
    x1 = x_ref[0] + gate_ref[0] * mix
    x1_ref[0] = x1
    ms = jnp.mean(x1 * x1, axis=-1, keepdims=True)
    y = x1 * lax.rsqrt(ms + EPS) * g_ref[...]
    h2_ref[0] = (y * (1.0 + scale_ref[0]) + shift_ref[0]).astype(BF16)


def _out_proj(fox, moba, x, wo_f, wo_m, gate, shift, scale, g):
    bsz, s, d = x.shape
    tm = ROW_TILE
    wf = fox.shape[2]
    wm = moba.shape[2]
    row = lambda b, t: (b, t, 0)
    per_b = lambda b, t: (b, 0, 0)
    return pl.pallas_call(
        _out_proj_kernel,
        grid=(bsz, s // tm),
        in_specs=[
            pl.BlockSpec((1, tm, wf), row),
            pl.BlockSpec((1, tm, wm), row),
            pl.BlockSpec((1, tm, d), row),
            _const_spec((wf, d)),
            _const_spec((wm, d)),
            pl.BlockSpec((1, 1, d), per_b),
            pl.BlockSpec((1, 1, d), per_b),
            pl.BlockSpec((1, 1, d), per_b),
            _const_spec((1, d)),
        ],
        out_specs=[pl.BlockSpec((1, tm, d), row), pl.BlockSpec((1, tm, d), row)],
        out_shape=[jax.ShapeDtypeStruct((bsz, s, d), F32), jax.ShapeDtypeStruct((bsz, s, d), BF16)],
        compiler_params=_params(("parallel", "parallel")),
        name="out_proj",
    )(fox, moba, x, wo_f, wo_m, gate, shift, scale, g)


def _ffn_kernel(h2_ref, x1_ref, wg_ref, wu_ref, wd_ref, gate_ref, o_ref):
    h2 = h2_ref[0]
    gt = _dot(h2, wg_ref[...])
    up = _dot(h2, wu_ref[...])
    act = (gt * jax.nn.sigmoid(gt) * up).astype(BF16)
    o_ref[0] = x1_ref[0] + gate_ref[0] * _dot(act, wd_ref[...])


def _ffn(h2, x1, wg, wu, wd, gate):
    bsz, s, d = x1.shape
    dff = wg.shape[1]
    tm = ROW_TILE
    row = lambda b, t: (b, t, 0)
    return pl.pallas_call(
        _ffn_kernel,
        grid=(bsz, s // tm),
        in_specs=[
            pl.BlockSpec((1, tm, d), row),
            pl.BlockSpec((1, tm, d), row),
            _const_spec((d, dff)),
            _const_spec((d, dff)),
            _const_spec((dff, d)),
            pl.BlockSpec((1, 1, d), lambda b, t: (b, 0, 0)),
        ],
        out_specs=pl.BlockSpec((1, tm, d), row),
        out_shape=jax.ShapeDtypeStruct((bsz, s, d), F32),
        compiler_params=_params(("parallel", "parallel")),
        name="ffn",
    )(h2, x1, wg, wu, wd, gate)


def kernel(x, c, w_ada, b_ada, norm1, norm2, w_in, b_forget, q_norm_fox, k_norm_fox,
           q_norm_moba, k_norm_moba, rel_bias, w_o, w_gate, w_up, w_down):
    bsz, s, d = x.shape
    depth = w_ada.shape[0]
    hf = b_forget.shape[1]
    hm = rel_bias.shape[1]
    n_buckets = rel_bias.shape[0]
    wf = hf * HEAD_DIM
    wm = hm * HEAD_DIM
    nb = s // KV_BLOCK
    assert q_norm_fox.shape[1] == HEAD_DIM and s % KV_BLOCK == 0
    assert wf % 256 == 0 and wm == wf and hf <= 128 and ROW_TILE == KV_BLOCK
    assert w_in.shape[2] == 3 * wf + hf + 3 * wm

    ind = jnp.asarray(np.kron(np.eye(256 // HEAD_DIM), np.ones((HEAD_DIM, HEAD_DIM))), BF16)
    tri = jnp.asarray(np.triu(np.ones((KV_BLOCK, KV_BLOCK))), BF16)
    rows = hf * nb
    rr = np.arange(rows)
    blk_lower = ((rr[:, None] // nb == rr[None, :] // nb) & (rr[None, :] % nb < rr[:, None] % nb))
    blk_lower = jnp.asarray(blk_lower.astype(np.float32), BF16)
    kl = np.arange(KV_BLOCK)[:, None]
    ql = np.arange(KV_BLOCK)[None, :]
    bkt = jnp.asarray(np.stack([_t5_bucket_np(ql - kl, n_buckets),
                                _t5_bucket_np(KV_BLOCK + ql - kl, n_buckets)]))
    assert KV_BLOCK + 1 >= MAX_DISTANCE

    tables = _t5_tables(rel_bias.T, bkt)
    far = rel_bias[n_buckets - 1, :]

    for l in range(depth):
        mod = _adaln(c, w_ada[l], b_ada[l]).reshape(bsz, 6, 1, d)
        shift1, scale1, gate1, shift2, scale2, gate2 = [mod[:, t] for t in range(6)]

        wl = w_in[l]
        o_ff = 3 * wf
        w_all = jnp.concatenate(
            [wl[:, :o_ff], wl[:, o_ff + hf:], wl[:, o_ff:o_ff + hf], jnp.zeros((d, 128 - hf), wl.dtype)],
            axis=1).astype(BF16)
        gains = jnp.stack([jnp.tile(q_norm_fox[l], hf), jnp.tile(k_norm_fox[l], hf),
                           jnp.tile(q_norm_moba[l], hm), jnp.tile(k_norm_moba[l], hm)])
        fq, fk, fv, mq, mk, mv, ff, kmean = _in_proj(
            x, shift1, scale1, norm1[l].reshape(1, d), w_all, ind, gains, wf, wm, hf)

        ff_rows = jnp.transpose(ff, (0, 2, 1)).reshape(bsz, rows, KV_BLOCK)
        b_rows = jnp.repeat(b_forget[l], nb).reshape(rows, 1)
        fcum = _forget(ff_rows, b_rows, tri, blk_lower).reshape(bsz, hf // 2, 2, nb, KV_BLOCK)
        frow = jnp.transpose(fcum, (0, 1, 3, 2, 4))
        fcol = jnp.transpose(fcum.reshape(bsz, hf // 2, 2, s), (0, 1, 3, 2))
        fox_out = _fox(fq, fk, fv, fcol, frow)

        mvt = jnp.transpose(mv.reshape(bsz, nb, KV_BLOCK, hm // 2, PAIR), (0, 3, 1, 4, 2))
        moba_out = _moba(far, mq, mk, mvt, kmean, tables)

        wo = w_o[l].astype(BF16)
        x1, h2 = _out_proj(fox_out, moba_out, x, wo[:wf], wo[wf:], gate1, shift2, scale2,
                           norm2[l].reshape(1, d))
        x = _ffn(h2, x1, w_gate[l].astype(BF16), w_up[l].astype(BF16), w_down[l].astype(BF16), gate2)
    return x
```

```python
import functools
import math

import numpy as np
import jax
import jax.numpy as jnp
from jax import lax
from jax.experimental import pallas as pl
from jax.experimental.pallas import tpu as pltpu

F32 = jnp.float32
BF16 = jnp.bfloat16

EPS = 1e-6
HEAD_DIM = 64
PAIR = 2 * HEAD_DIM
KV_BLOCK = 256
MOBA_TOPK = 3
MAX_DISTANCE = 128
ROW_TILE = 256
VMEM_LIMIT = 56 * 1024 * 1024
NEG_INF = float("-inf")


def _split3(x):
    hi = x.astype(BF16)
    r = x - hi.astype(F32)
    mid = r.astype(BF16)
    lo = (r - mid.astype(F32)).astype(BF16)
    return hi, mid, lo


def _dot(a, b):
    return jnp.dot(a, b, preferred_element_type=F32)


def _dot_nt(a, b):
    return lax.dot_general(a, b, (((1,), (1,)), ((), ())), preferred_element_type=F32)


def _params(sem):
    return pltpu.CompilerParams(dimension_semantics=sem, vmem_limit_bytes=VMEM_LIMIT)


def _const_spec(shape):
    nd = len(shape)
    return pl.BlockSpec(shape, lambda *_: (0,) * nd, pipeline_mode=pl.Buffered(1))


def _adaln_kernel(c_ref, w_ref, b_ref, o_ref):
    c = c_ref[...]
    a = c * jax.nn.sigmoid(c)
    w = w_ref[...]
    a_hi, a_mid, a_lo = _split3(a)
    w_hi, w_mid, w_lo = _split3(w)
    acc = _dot(a_lo, w_hi) + _dot(a_mid, w_mid) + _dot(a_hi, w_lo)
    acc = acc + _dot(a_mid, w_hi) + _dot(a_hi, w_mid)
    acc = acc + _dot(a_hi, w_hi)
    o_ref[...] = acc + b_ref[...]


def _adaln(c, w, b):
    bsz, d = c.shape
    n = w.shape[1]
    tn = 1024 if n % 1024 == 0 else n
    return pl.pallas_call(
        _adaln_kernel,
        grid=(n // tn,),
        in_specs=[
            pl.BlockSpec((bsz, d), lambda i: (0, 0)),
            pl.BlockSpec((d, tn), lambda i: (0, i)),
            pl.BlockSpec((1, tn), lambda i: (0, i)),
        ],
        out_specs=pl.BlockSpec((bsz, tn), lambda i: (0, i)),
        out_shape=jax.ShapeDtypeStruct((bsz, n), F32),
        compiler_params=_params(("parallel",)),
        name="adaln",
    )(c, w, b.reshape(1, n))


def _head_rms(y, ind, gain):
    cols = []
    for c0 in range(0, y.shape[1], 256):
        yc = y[:, c0:c0 + 256]
        sq = yc * yc
        sq_hi = sq.astype(BF16)
        sq_lo = (sq - sq_hi.astype(F32)).astype(BF16)
        ss = _dot(sq_hi, ind) + _dot(sq_lo, ind)
        cols.append(yc * lax.rsqrt(ss * (1.0 / HEAD_DIM) + EPS) * gain[:, c0:c0 + 256])
    return jnp.concatenate(cols, axis=1)


def _in_proj_kernel(x_ref, shift_ref, scale_ref, g_ref, w_ref, wvt_ref, ind_ref, gains_ref,
                    fq_ref, fk_ref, mq_ref, mk_ref, fvt_ref, mvt_ref, ff_ref, kmean_ref, *, wf, wm):
    t = pl.program_id(1)
    x = x_ref[0]
    ms = jnp.mean(x * x, axis=-1, keepdims=True)
    y = x * lax.rsqrt(ms + EPS) * g_ref[...]
    h = (y * (1.0 + scale_ref[0]) + shift_ref[0]).astype(BF16)
    proj = _dot(h, w_ref[...])
    vt = _dot_nt(wvt_ref[...], h)
    ind = ind_ref[...]
    gains = gains_ref[...]
    o = 0
    fq = _head_rms(proj[:, o:o + wf], ind, gains[0:1]); o += wf
    fk = _head_rms(proj[:, o:o + wf], ind, gains[1:2]); o += wf
    mq = _head_rms(proj[:, o:o + wm], ind, gains[2:3]); o += wm
    mk = _head_rms(proj[:, o:o + wm], ind, gains[3:4]); o += wm
    ff = proj[:, o:o + ff_ref.shape[2]]
    fq_ref[0] = fq.astype(BF16)
    fk_ref[0] = fk.astype(BF16)
    mq_ref[0] = mq.astype(BF16)
    mk_ref[0] = mk.astype(BF16)
    fvt_ref[0] = vt[:wf].astype(BF16)
    mvt_ref[0] = vt[wf:].astype(BF16)
    ff_ref[0] = ff
    kmean_ref[0, pl.ds(t, 1), :] = jnp.sum(mk, axis=0, keepdims=True) * (1.0 / KV_BLOCK)


def _in_proj(x, shift, scale, g, w_all, wvt, ind, gains, wf, wm, hf):
    bsz, s, d = x.shape
    tm = ROW_TILE
    nb = s // KV_BLOCK
    row = lambda b, t: (b, t, 0)
    col = lambda b, t: (b, 0, t)
    per_b = lambda b, t: (b, 0, 0)
    return pl.pallas_call(
        functools.partial(_in_proj_kernel, wf=wf, wm=wm),
        grid=(bsz, s // tm),
        in_specs=[
            pl.BlockSpec((1, tm, d), row),
            pl.BlockSpec((1, 1, d), per_b),
            pl.BlockSpec((1, 1, d), per_b),
            _const_spec((1, d)),
            _const_spec(w_all.shape),
            _const_spec(wvt.shape),
            _const_spec((256, 256)),
            _const_spec(gains.shape),
        ],
        out_specs=[
            pl.BlockSpec((1, tm, wf), row),
            pl.BlockSpec((1, tm, wf), row),
            pl.BlockSpec((1, tm, wm), row),
            pl.BlockSpec((1, tm, wm), row),
            pl.BlockSpec((1, wf, tm), col),
            pl.BlockSpec((1, wm, tm), col),
            pl.BlockSpec((1, tm, hf), row),
            pl.BlockSpec((1, nb, wm), per_b),
        ],
        out_shape=[
            jax.ShapeDtypeStruct((bsz, s, wf), BF16),
            jax.ShapeDtypeStruct((bsz, s, wf), BF16),
            jax.ShapeDtypeStruct((bsz, s, wm), BF16),
            jax.ShapeDtypeStruct((bsz, s, wm), BF16),
            jax.ShapeDtypeStruct((bsz, wf, s), BF16),
            jax.ShapeDtypeStruct((bsz, wm, s), BF16),
            jax.ShapeDtypeStruct((bsz, s, hf), F32),
            jax.ShapeDtypeStruct((bsz, nb, wm), F32),
        ],
        compiler_params=_params(("parallel", "arbitrary")),
        name="in_proj",
    )(x, shift, scale, g, w_all, wvt, ind, gains)


def _forget_kernel(ff_ref, b_ref, tri_ref, blk_ref, o_ref):
    z = ff_ref[0] + b_ref[...]
    lf = jnp.minimum(z, 0.0) - jnp.log1p(jnp.exp(-jnp.abs(z)))
    tri = tri_ref[...]
    hi, mid, lo = _split3(lf)
    local = _dot(lo, tri) + _dot(mid, tri) + _dot(hi, tri)
    tot = jnp.broadcast_to(local[:, KV_BLOCK - 1:KV_BLOCK], (local.shape[0], 128))
    blk = blk_ref[...]
    t_hi, t_mid, t_lo = _split3(tot)
    off = _dot(blk, t_lo) + _dot(blk, t_mid) + _dot(blk, t_hi)
    o_ref[0] = local + off[:, 0:1]


def _forget(ff_rows, b_rows, tri, blk):
    bsz, rows, _ = ff_rows.shape
    return pl.pallas_call(
        _forget_kernel,
        grid=(bsz,),
        in_specs=[
            pl.BlockSpec((1, rows, KV_BLOCK), lambda b: (b, 0, 0)),
            pl.BlockSpec((rows, 1), lambda b: (0, 0)),
            pl.BlockSpec((KV_BLOCK, KV_BLOCK), lambda b: (0, 0)),
            pl.BlockSpec((rows, rows), lambda b: (0, 0)),
        ],
        out_specs=pl.BlockSpec((1, rows, KV_BLOCK), lambda b: (b, 0, 0)),
        out_shape=jax.ShapeDtypeStruct((bsz, rows, KV_BLOCK), F32),
        compiler_params=_params(("parallel",)),
        name="forget",
    )(ff_rows, b_rows, tri, blk)


def _t5_bucket_np(dist, n_buckets):
    n = np.maximum(dist, 0)
    max_exact = n_buckets // 2
    nf = np.maximum(n, 1).astype(np.float32)
    large = max_exact + (np.log(nf / np.float32(max_exact)) / np.float32(math.log(MAX_DISTANCE / max_exact))
                         * np.float32(n_buckets - max_exact)).astype(np.int32)
    large = np.minimum(large, n_buckets - 1)
    return np.where(n < max_exact, n, large).astype(np.int32)


def _t5_kernel(relb_ref, bkt_ref, o_ref, *, n_buckets):
    h = pl.program_id(0)
    kl = lax.broadcasted_iota(jnp.int32, (KV_BLOCK, KV_BLOCK), 0)
    ql = lax.broadcasted_iota(jnp.int32, (KV_BLOCK, KV_BLOCK), 1)
    for t in range(2):
        b = bkt_ref[t]
        acc = jnp.zeros((KV_BLOCK, KV_BLOCK), F32)
        for u in range(n_buckets):
            acc = jnp.where(b == u, relb_ref[h, u], acc)
        if t == 0:
            acc = jnp.where(kl <= ql, acc, NEG_INF)
        o_ref[0, t] = acc


def _t5_tables(rel_bias_t, bkt):
    nh, n_buckets = rel_bias_t.shape
    return pl.pallas_call(
        functools.partial(_t5_kernel, n_buckets=n_buckets),
        grid=(nh,),
        in_specs=[
            pl.BlockSpec(memory_space=pltpu.SMEM),
            pl.BlockSpec((2, KV_BLOCK, KV_BLOCK), lambda h: (0, 0, 0)),
        ],
        out_specs=pl.BlockSpec((1, 2, KV_BLOCK, KV_BLOCK), lambda h: (h, 0, 0, 0)),
        out_shape=jax.ShapeDtypeStruct((nh, 2, KV_BLOCK, KV_BLOCK), F32),
        compiler_params=_params(("parallel",)),
        name="t5_tables",
    )(rel_bias_t, bkt)


def _colmax(x):
    return jnp.max(x, axis=0, keepdims=True)


def _colsum(x):
    return jnp.sum(x, axis=0, keepdims=True)


def _head_masks(rows):
    lane = lax.broadcasted_iota(jnp.int32, (rows, PAIR), 1)
    return [lane < HEAD_DIM, lane >= HEAD_DIM]


def _fox_kernel(q_ref, k_ref, vt_ref, fcol_ref, frow_ref, o_ref):
    s_len = q_ref.shape[1]
    tq = KV_BLOCK
    heads = _head_masks(tq)
    orow = lax.broadcasted_iota(jnp.int32, (PAIR, tq), 0)
    kl = lax.broadcasted_iota(jnp.int32, (tq, tq), 0)
    ql = lax.broadcasted_iota(jnp.int32, (tq, tq), 1)
    causal = kl <= ql
    fcol = fcol_ref[0, 0]
    fk_b = [jnp.broadcast_to(fcol[:, hh:hh + 1], (s_len, tq)) for hh in range(2)]
    for i in range(s_len // tq):
        lo, hi = i * tq, (i + 1) * tq
        qf = q_ref[0, lo:hi, :].astype(F32) * (HEAD_DIM ** -0.5)
        outs = []
        for hh in range(2):
            qh = jnp.where(heads[hh], qf, 0.0).astype(BF16)
            fq = frow_ref[0, 0, hh:hh + 1, lo:hi]
            t_own = jnp.where(causal, _dot_nt(k_ref[0, lo:hi, :], qh) - fk_b[hh][lo:hi], NEG_INF)
            mt = _colmax(t_own)
            if i > 0:
                t_far = _dot_nt(k_ref[0, 0:lo, :], qh) - fk_b[hh][0:lo]
                mt = jnp.maximum(mt, _colmax(t_far))
            m = mt + fq
            shift = m - fq
            p_own = jnp.exp(t_own - shift)
            l = _colsum(p_own)
            acc = _dot(vt_ref[0, :, lo:hi], p_own.astype(BF16))
            if i > 0:
                p_far = jnp.exp(t_far - shift)
                l = l + _colsum(p_far)
                acc = acc + _dot(vt_ref[0, :, 0:lo], p_far.astype(BF16))
            outs.append(acc * (1.0 / l))
        out_t = jnp.where(orow < HEAD_DIM, outs[0], outs[1])
        o_ref[0, lo:hi, :] = out_t.T.astype(o_ref.dtype)


def _fox(q, k, vt, fcol, frow):
    bsz, s, w = q.shape
    npair = w // PAIR
    return pl.pallas_call(
        _fox_kernel,
        grid=(bsz, npair),
        in_specs=[
            pl.BlockSpec((1, s, PAIR), lambda b, j: (b, 0, j)),
            pl.BlockSpec((1, s, PAIR), lambda b, j: (b, 0, j)),
            pl.BlockSpec((1, PAIR, s), lambda b, j: (b, j, 0)),
            pl.BlockSpec((1, 1, s, 2), lambda b, j: (b, j, 0, 0)),
            pl.BlockSpec((1, 1, 2, s), lambda b, j: (b, j, 0, 0)),
        ],
        out_specs=pl.BlockSpec((1, s, PAIR), lambda b, j: (b, 0, j)),
        out_shape=jax.ShapeDtypeStruct((bsz, s, w), BF16),
        compiler_params=_params(("parallel", "parallel")),
        name="fox",
    )(q, k, vt, fcol, frow)


def _moba_kernel(far_ref, q_ref, k_ref, vt_ref, kmean_ref, tbl_ref, o_ref):
    j = pl.program_id(1)
    s_len = q_ref.shape[1]
    tq = KV_BLOCK
    nb = kmean_ref.shape[1]
    heads = _head_masks(tq)
    kheads = _head_masks(nb)
    blk = lax.broadcasted_iota(jnp.int32, (nb, tq), 0)
    orow = lax.broadcasted_iota(jnp.int32, (PAIR, tq), 0)
    kmean = kmean_ref[0]
    km_parts = [_split3(jnp.where(kheads[hh], kmean, 0.0)) for hh in range(2)]
    for i in range(s_len // tq):
        lo, hi = i * tq, (i + 1) * tq
        qf = q_ref[0, lo:hi, :].astype(F32)
        outs = []
        for hh in range(2):
            qh = jnp.where(heads[hh], qf * (HEAD_DIM ** -0.5), 0.0).astype(BF16)
            far = far_ref[2 * j + hh]
            s_own = _dot_nt(k_ref[0, lo:hi, :], qh) + tbl_ref[hh, 0]
            mt = _colmax(s_own)
            biases = []
            if i > 0:
                q_gate = jnp.where(heads[hh], qf, 0.0).astype(BF16)
                km_hi, km_mid, km_lo = km_parts[hh]
                g = _dot_nt(km_lo, q_gate) + _dot_nt(km_mid, q_gate) + _dot_nt(km_hi, q_gate)
                valid = blk < i
                g = jnp.where(valid, g, NEG_INF)
                rank = jnp.zeros((nb, tq), jnp.int32)
                for mm in range(i):
                    gm = g[mm:mm + 1, :]
                    beats = (gm > g) | ((gm == g) & (blk > mm))
                    rank = rank + beats.astype(jnp.int32)
                sel_bias = jnp.where(valid & (rank < MOBA_TOPK), 0.0, NEG_INF)
                s_prev = _dot_nt(k_ref[0, lo - tq:lo, :], qh) + tbl_ref[hh, 1]
                b_prev = sel_bias[i - 1:i, :]
                mt = jnp.maximum(mt, _colmax(s_prev) + b_prev)
            if i > 1:
                s_far = _dot_nt(k_ref[0, 0:lo - tq, :], qh)
                for n in range(i - 1):
                    b_n = sel_bias[n:n + 1, :] + far
                    biases.append(b_n)
                    mt = jnp.maximum(mt, _colmax(s_far[n * tq:(n + 1) * tq]) + b_n)
            p = jnp.exp(s_own - mt)
            l = _colsum(p)
            acc = _dot(vt_ref[0, :, lo:hi], p.astype(BF16))
            if i > 0:
                p = jnp.exp(s_prev - (mt - b_prev))
                l = l + _colsum(p)
                acc = acc + _dot(vt_ref[0, :, lo - tq:lo], p.astype(BF16))
            if i > 1:
                for n in range(i - 1):
                    p = jnp.exp(s_far[n * tq:(n + 1) * tq] - (mt - biases[n]))
                    l = l + _colsum(p)
                    acc = acc + _dot(vt_ref[0, :, n * tq:(n + 1) * tq], p.astype(BF16))
            outs.append(acc * (1.0 / l))
        out_t = jnp.where(orow < HEAD_DIM, outs[0], outs[1])
        o_ref[0, lo:hi, :] = out_t.T.astype(o_ref.dtype)


def _moba(far, q, k, vt, kmean, tbl):
    bsz, s, w = q.shape
    npair = w // PAIR
    nb = s // KV_BLOCK
    return pl.pallas_call(
        _moba_kernel,
        grid=(bsz, npair),
        in_specs=[
            pl.BlockSpec(memory_space=pltpu.SMEM),
            pl.BlockSpec((1, s, PAIR), lambda b, j: (b, 0, j)),
            pl.BlockSpec((1, s, PAIR), lambda b, j: (b, 0, j)),
            pl.BlockSpec((1, PAIR, s), lambda b, j: (b, j, 0)),
            pl.BlockSpec((1, nb, PAIR), lambda b, j: (b, 0, j)),
            pl.BlockSpec((2, 2, KV_BLOCK, KV_BLOCK), lambda b, j: (j, 0, 0, 0)),
        ],
        out_specs=pl.BlockSpec((1, s, PAIR), lambda b, j: (b, 0, j)),
        out_shape=jax.ShapeDtypeStruct((bsz, s, w), BF16),
        compiler_params=_params(("parallel", "parallel")),
        name="moba",
    )(far, q, k, vt, kmean, tbl)


def _out_proj_kernel(fox_ref, moba_ref, x_ref, wof_ref, wom_ref, gate_ref, shift_ref, scale_ref, g_ref,
                     x1_ref, h2_ref):
    mix = _dot(fox_ref[0], wof_ref[...]) + _dot(moba_ref[0], wom_ref[...])
    x1 = x_ref[0] + gate_ref[0] * mix
    x1_ref[0] = x1
    ms = jnp.mean(x1 * x1, axis=-1, keepdims=True)
    y = x1 * lax.rsqrt(ms + EPS) * g_ref[...]
    h2_ref[0] = (y * (1.0 + scale_ref[0]) + shift_ref[0]).astype(BF16)


def _out_proj(fox, moba, x, wo_f, wo_m, gate, shift, scale, g):
    bsz, s, d = x.shape
    tm = ROW_TILE
    wf = fox.shape[2]
    wm = moba.shape[2]
    row = lambda b, t: (b, t, 0)
    per_b = lambda b, t: (b, 0, 0)
    return pl.pallas_call(
        _out_proj_kernel,
        grid=(bsz, s // tm),
        in_specs=[
            pl.BlockSpec((1, tm, wf), row),
            pl.BlockSpec((1, tm, wm), row),
            pl.BlockSpec((1, tm, d), row),
            _const_spec((wf, d)),
            _const_spec((wm, d)),
            pl.BlockSpec((1, 1, d), per_b),
            pl.BlockSpec((1, 1, d), per_b),
            pl.BlockSpec((1, 1, d), per_b),
            _const_spec((1, d)),
        ],
        out_specs=[pl.BlockSpec((1, tm, d), row), pl.BlockSpec((1, tm, d), row)],
        out_shape=[jax.ShapeDtypeStruct((bsz, s, d), F32), jax.ShapeDtypeStruct((bsz, s, d), BF16)],
        compiler_params=_params(("parallel", "parallel")),
        name="out_proj",
    )(fox, moba, x, wo_f, wo_m, gate, shift, scale, g)


def _ffn_kernel(h2_ref, x1_ref, wg_ref, wu_ref, wd_ref, gate_ref, o_ref):
    h2 = h2_ref[0]
    gt = _dot(h2, wg_ref[...])
    up = _dot(h2, wu_ref[...])
    act = (gt * jax.nn.sigmoid(gt) * up).astype(BF16)
    o_ref[0] = x1_ref[0] + gate_ref[0] * _dot(act, wd_ref[...])


def _ffn(h2, x1, wg, wu, wd, gate):
    bsz, s, d = x1.shape
    dff = wg.shape[1]
    tm = ROW_TILE
    row = lambda b, t: (b, t, 0)
    return pl.pallas_call(
        _ffn_kernel,
        grid=(bsz, s // tm),
        in_specs=[
            pl.BlockSpec((1, tm, d), row),
            pl.BlockSpec((1, tm, d), row),
            _const_spec((d, dff)),
            _const_spec((d, dff)),
            _const_spec((dff, d)),
            pl.BlockSpec((1, 1, d), lambda b, t: (b, 0, 0)),
        ],
        out_specs=pl.BlockSpec((1, tm, d), row),
        out_shape=jax.ShapeDtypeStruct((bsz, s, d), F32),
        compiler_params=_params(("parallel", "parallel")),
        name="ffn",
    )(h2, x1, wg, wu, wd, gate)


def kernel(x, c, w_ada, b_ada, norm1, norm2, w_in, b_forget, q_norm_fox, k_norm_fox,
           q_norm_moba, k_norm_moba, rel_bias, w_o, w_gate, w_up, w_down):
    bsz, s, d = x.shape
    depth = w_ada.shape[0]
    hf = b_forget.shape[1]
    hm = rel_bias.shape[1]
    n_buckets = rel_bias.shape[0]
    wf = hf * HEAD_DIM
    wm = hm * HEAD_DIM
    nb = s // KV_BLOCK
    assert q_norm_fox.shape[1] == HEAD_DIM and s % KV_BLOCK == 0
    assert wf % 256 == 0 and wm == wf and hf <= 128 and ROW_TILE == KV_BLOCK
    assert w_in.shape[2] == 3 * wf + hf + 3 * wm

    ind = jnp.asarray(np.kron(np.eye(256 // HEAD_DIM), np.ones((HEAD_DIM, HEAD_DIM))), BF16)
    tri = jnp.asarray(np.triu(np.ones((KV_BLOCK, KV_BLOCK))), BF16)
    rows = hf * nb
    rr = np.arange(rows)
    blk_lower = ((rr[:, None] // nb == rr[None, :] // nb) & (rr[None, :] % nb < rr[:, None] % nb))
    blk_lower = jnp.asarray(blk_lower.astype(np.float32), BF16)
    kl = np.arange(KV_BLOCK)[:, None]
    ql = np.arange(KV_BLOCK)[None, :]
    bkt = jnp.asarray(np.stack([_t5_bucket_np(ql - kl, n_buckets),
                                _t5_bucket_np(KV_BLOCK + ql - kl, n_buckets)]))
    assert KV_BLOCK + 1 >= MAX_DISTANCE

    tables = _t5_tables(rel_bias.T, bkt)
    far = rel_bias[n_buckets - 1, :]

    for l in range(depth):
        mod = _adaln(c, w_ada[l], b_ada[l]).reshape(bsz, 6, 1, d)
        shift1, scale1, gate1, shift2, scale2, gate2 = [mod[:, t] for t in range(6)]

        wl = w_in[l]
        o_fv, o_ff, o_mq = 2 * wf, 3 * wf, 3 * wf + hf
        w_all = jnp.concatenate(
            [wl[:, :o_fv], wl[:, o_mq:o_mq + 2 * wm], wl[:, o_ff:o_ff + hf], jnp.zeros((d, 128 - hf), wl.dtype)],
            axis=1).astype(BF16)
        wvt = jnp.concatenate([wl[:, o_fv:o_ff], wl[:, o_mq + 2 * wm:]], axis=1).T.astype(BF16)
        gains = jnp.stack([jnp.tile(q_norm_fox[l], hf), jnp.tile(k_norm_fox[l], hf),
                           jnp.tile(q_norm_moba[l], hm), jnp.tile(k_norm_moba[l], hm)])
        fq, fk, mq, mk, fvt, mvt, ff, kmean = _in_proj(
            x, shift1, scale1, norm1[l].reshape(1, d), w_all, wvt, ind, gains, wf, wm, hf)

        ff_rows = jnp.transpose(ff, (0, 2, 1)).reshape(bsz, rows, KV_BLOCK)
        b_rows = jnp.repeat(b_forget[l], nb).reshape(rows, 1)
        frow = _forget(ff_rows, b_rows, tri, blk_lower).reshape(bsz, hf // 2, 2, s)
        fcol = jnp.transpose(frow, (0, 1, 3, 2))
        fox_out = _fox(fq, fk, fvt, fcol, frow)
        moba_out = _moba(far, mq, mk, mvt, kmean, tables)

        wo = w_o[l].astype(BF16)
        x1, h2 = _out_proj(fox_out, moba_out, x, wo[:wf], wo[wf:], gate1, shift2, scale2,
                           norm2[l].reshape(1, d))
        x = _ffn(h2, x1, w_gate[l].astype(BF16), w_up[l].astype(BF16), w_down[l].astype(BF16), gate2)
    return x
```

```python
import functools
import math

import numpy as np
import jax
import jax.numpy as jnp
from jax import lax
from jax.experimental import pallas as pl
from jax.experimental.pallas import tpu as pltpu

F32 = jnp.float32
BF16 = jnp.bfloat16

EPS = 1e-6
HEAD_DIM = 64
PAIR = 2 * HEAD_DIM
KV_BLOCK = 256
MOBA_TOPK = 3
MAX_DISTANCE = 128
ROW_TILE = 256
VMEM_LIMIT = 56 * 1024 * 1024
NEG_INF = float("-inf")
LOG2E = 1.4426950408889634
FOX_TQ = 512


def _split3(x):
    hi = x.astype(BF16)
    r = x - hi.astype(F32)
    mid = r.astype(BF16)
    lo = (r - mid.astype(F32)).astype(BF16)
    return hi, mid, lo


def _dot(a, b):
    return jnp.dot(a, b, preferred_element_type=F32)


def _dot_nt(a, b):
    return lax.dot_general(a, b, (((1,), (1,)), ((), ())), preferred_element_type=F32)


def _params(sem):
    return pltpu.CompilerParams(dimension_semantics=sem, vmem_limit_bytes=VMEM_LIMIT)


def _const_spec(shape):
    nd = len(shape)
    return pl.BlockSpec(shape, lambda *_: (0,) * nd, pipeline_mode=pl.Buffered(1))


def _adaln_kernel(c_ref, w_ref, b_ref, o_ref):
    c = c_ref[...]
    a = c * jax.nn.sigmoid(c)
    w = w_ref[...]
    a_hi, a_mid, a_lo = _split3(a)
    w_hi, w_mid, w_lo = _split3(w)
    acc = _dot(a_lo, w_hi) + _dot(a_mid, w_mid) + _dot(a_hi, w_lo)
    acc = acc + _dot(a_mid, w_hi) + _dot(a_hi, w_mid)
    acc = acc + _dot(a_hi, w_hi)
    o_ref[...] = acc + b_ref[...]


def _adaln(c, w, b):
    bsz, d = c.shape
    n = w.shape[1]
    tn = 1024 if n % 1024 == 0 else n
    return pl.pallas_call(
        _adaln_kernel,
        grid=(n // tn,),
        in_specs=[
            pl.BlockSpec((bsz, d), lambda i: (0, 0)),
            pl.BlockSpec((d, tn), lambda i: (0, i)),
            pl.BlockSpec((1, tn), lambda i: (0, i)),
        ],
        out_specs=pl.BlockSpec((bsz, tn), lambda i: (0, i)),
        out_shape=jax.ShapeDtypeStruct((bsz, n), F32),
        compiler_params=_params(("parallel",)),
        name="adaln",
    )(c, w, b.reshape(1, n))


def _head_rms(y, ind, gain):
    cols = []
    for c0 in range(0, y.shape[1], 256):
        yc = y[:, c0:c0 + 256]
        sq = yc * yc
        sq_hi = sq.astype(BF16)
        sq_lo = (sq - sq_hi.astype(F32)).astype(BF16)
        ss = _dot(sq_hi, ind) + _dot(sq_lo, ind)
        cols.append(yc * lax.rsqrt(ss * (1.0 / HEAD_DIM) + EPS) * gain[:, c0:c0 + 256])
    return jnp.concatenate(cols, axis=1)


def _in_proj_kernel(x_ref, shift_ref, scale_ref, g_ref, w_ref, wvt_ref, ind_ref, gains_ref,
                    fq_ref, fk_ref, mq_ref, mk_ref, fvt_ref, mvt_ref, ff_ref, kmean_ref, *, wf, wm):
    t = pl.program_id(1)
    x = x_ref[0]
    ms = jnp.mean(x * x, axis=-1, keepdims=True)
    y = x * lax.rsqrt(ms + EPS) * g_ref[...]
    h = (y * (1.0 + scale_ref[0]) + shift_ref[0]).astype(BF16)
    proj = _dot(h, w_ref[...])
    vt = _dot_nt(wvt_ref[...], h)
    ind = ind_ref[...]
    gains = gains_ref[...]
    o = 0
    fq = _head_rms(proj[:, o:o + wf], ind, gains[0:1]); o += wf
    fk = _head_rms(proj[:, o:o + wf], ind, gains[1:2]); o += wf
    mq = _head_rms(proj[:, o:o + wm], ind, gains[2:3]); o += wm
    mk = _head_rms(proj[:, o:o + wm], ind, gains[3:4]); o += wm
    ff = proj[:, o:o + ff_ref.shape[2]]
    fq_ref[0] = fq.astype(BF16)
    fk_ref[0] = fk.astype(BF16)
    mq_ref[0] = mq.astype(BF16)
    mk_ref[0] = mk.astype(BF16)
    fvt_ref[0] = vt[:wf].astype(BF16)
    mvt_ref[0] = vt[wf:].astype(BF16)
    ff_ref[0] = ff
    kmean_ref[0, pl.ds(t, 1), :] = jnp.sum(mk, axis=0, keepdims=True) * (1.0 / KV_BLOCK)


def _in_proj(x, shift, scale, g, w_all, wvt, ind, gains, wf, wm, hf):
    bsz, s, d = x.shape
    tm = ROW_TILE
    nb = s // KV_BLOCK
    row = lambda b, t: (b, t, 0)
    col = lambda b, t: (b, 0, t)
    per_b = lambda b, t: (b, 0, 0)
    return pl.pallas_call(
        functools.partial(_in_proj_kernel, wf=wf, wm=wm),
        grid=(bsz, s // tm),
        in_specs=[
            pl.BlockSpec((1, tm, d), row),
            pl.BlockSpec((1, 1, d), per_b),
            pl.BlockSpec((1, 1, d), per_b),
            _const_spec((1, d)),
            _const_spec(w_all.shape),
            _const_spec(wvt.shape),
            _const_spec((256, 256)),
            _const_spec(gains.shape),
        ],
        out_specs=[
            pl.BlockSpec((1, tm, wf), row),
            pl.BlockSpec((1, tm, wf), row),
            pl.BlockSpec((1, tm, wm), row),
            pl.BlockSpec((1, tm, wm), row),
            pl.BlockSpec((1, wf, tm), col),
            pl.BlockSpec((1, wm, tm), col),
            pl.BlockSpec((1, tm, hf), row),
            pl.BlockSpec((1, nb, wm), per_b),
        ],
        out_shape=[
            jax.ShapeDtypeStruct((bsz, s, wf), BF16),
            jax.ShapeDtypeStruct((bsz, s, wf), BF16),
            jax.ShapeDtypeStruct((bsz, s, wm), BF16),
            jax.ShapeDtypeStruct((bsz, s, wm), BF16),
            jax.ShapeDtypeStruct((bsz, wf, s), BF16),
            jax.ShapeDtypeStruct((bsz, wm, s), BF16),
            jax.ShapeDtypeStruct((bsz, s, hf), F32),
            jax.ShapeDtypeStruct((bsz, nb, wm), F32),
        ],
        compiler_params=_params(("parallel", "arbitrary")),
        name="in_proj",
    )(x, shift, scale, g, w_all, wvt, ind, gains)


def _forget_kernel(ff_ref, b_ref, tri_ref, blk_ref, o_ref):
    z = ff_ref[0] + b_ref[...]
    lf = jnp.minimum(z, 0.0) - jnp.log1p(jnp.exp(-jnp.abs(z)))
    tri = tri_ref[...]
    hi, mid, lo = _split3(lf)
    local = _dot(lo, tri) + _dot(mid, tri) + _dot(hi, tri)
    tot = jnp.broadcast_to(local[:, KV_BLOCK - 1:KV_BLOCK], (local.shape[0], 128))
    blk = blk_ref[...]
    t_hi, t_mid, t_lo = _split3(tot)
    off = _dot(blk, t_lo) + _dot(blk, t_mid) + _dot(blk, t_hi)
    o_ref[0] = local + off[:, 0:1]


def _forget(ff_rows, b_rows, tri, blk):
    bsz, rows, _ = ff_rows.shape
    return pl.pallas_call(
        _forget_kernel,
        grid=(bsz,),
        in_specs=[
            pl.BlockSpec((1, rows, KV_BLOCK), lambda b: (b, 0, 0)),
            pl.BlockSpec((rows, 1), lambda b: (0, 0)),
            pl.BlockSpec((KV_BLOCK, KV_BLOCK), lambda b: (0, 0)),
            pl.BlockSpec((rows, rows), lambda b: (0, 0)),
        ],
        out_specs=pl.BlockSpec((1, rows, KV_BLOCK), lambda b: (b, 0, 0)),
        out_shape=jax.ShapeDtypeStruct((bsz, rows, KV_BLOCK), F32),
        compiler_params=_params(("parallel",)),
        name="forget",
    )(ff_rows, b_rows, tri, blk)


def _t5_bucket_np(dist, n_buckets):
    n = np.maximum(dist, 0)
    max_exact = n_buckets // 2
    nf = np.maximum(n, 1).astype(np.float32)
    large = max_exact + (np.log(nf / np.float32(max_exact)) / np.float32(math.log(MAX_DISTANCE / max_exact))
                         * np.float32(n_buckets - max_exact)).astype(np.int32)
    large = np.minimum(large, n_buckets - 1)
    return np.where(n < max_exact, n, large).astype(np.int32)


def _t5_kernel(relb_ref, bkt_ref, o_ref, *, n_buckets):
    h = pl.program_id(0)
    kl = lax.broadcasted_iota(jnp.int32, (KV_BLOCK, KV_BLOCK), 0)
    ql = lax.broadcasted_iota(jnp.int32, (KV_BLOCK, KV_BLOCK), 1)
    for t in range(2):
        b = bkt_ref[t]
        acc = jnp.zeros((KV_BLOCK, KV_BLOCK), F32)
        for u in range(n_buckets):
            acc = jnp.where(b == u, relb_ref[h, u], acc)
        if t == 0:
            acc = jnp.where(kl <= ql, acc, NEG_INF)
        o_ref[0, t] = acc * LOG2E


def _t5_tables(rel_bias_t, bkt):
    nh, n_buckets = rel_bias_t.shape
    return pl.pallas_call(
        functools.partial(_t5_kernel, n_buckets=n_buckets),
        grid=(nh,),
        in_specs=[
            pl.BlockSpec(memory_space=pltpu.SMEM),
            pl.BlockSpec((2, KV_BLOCK, KV_BLOCK), lambda h: (0, 0, 0)),
        ],
        out_specs=pl.BlockSpec((1, 2, KV_BLOCK, KV_BLOCK), lambda h: (h, 0, 0, 0)),
        out_shape=jax.ShapeDtypeStruct((nh, 2, KV_BLOCK, KV_BLOCK), F32),
        compiler_params=_params(("parallel",)),
        name="t5_tables",
    )(rel_bias_t, bkt)


def _colmax(x):
    return jnp.max(x, axis=0, keepdims=True)


def _colsum(x):
    return jnp.sum(x, axis=0, keepdims=True)


def _head_masks(rows):
    lane = lax.broadcasted_iota(jnp.int32, (rows, PAIR), 1)
    return [lane < HEAD_DIM, lane >= HEAD_DIM]


def _pipelined(n_tiles, first_stage, second_stage):
    units = [(i, hh) for i in range(n_tiles) for hh in range(2)]
    res = {}
    pending = None
    for u in units:
        state = first_stage(*u)
        if pending is not None:
            res[pending[0]] = second_stage(*pending[0], pending[1])
        pending = (u, state)
    res[pending[0]] = second_stage(*pending[0], pending[1])
    return res


def _fox_kernel(q_ref, k_ref, vt_ref, fcol_ref, frow_ref, o_ref):
    s_len = q_ref.shape[1]
    tq = FOX_TQ
    heads = _head_masks(tq)
    orow = lax.broadcasted_iota(jnp.int32, (PAIR, tq), 0)
    kl = lax.broadcasted_iota(jnp.int32, (tq, tq), 0)
    ql = lax.broadcasted_iota(jnp.int32, (tq, tq), 1)
    causal = kl <= ql
    fcol = fcol_ref[0, 0] * LOG2E
    fk_b = [jnp.broadcast_to(fcol[:, hh:hh + 1], (s_len, tq)) for hh in range(2)]

    def scores(i, hh):
        lo, hi = i * tq, (i + 1) * tq
        qf = q_ref[0, lo:hi, :].astype(F32) * (LOG2E * HEAD_DIM ** -0.5)
        qh = jnp.where(heads[hh], qf, 0.0).astype(BF16)
        fq = frow_ref[0, 0, hh:hh + 1, lo:hi] * LOG2E
        t_own = jnp.where(causal, _dot_nt(k_ref[0, lo:hi, :], qh) - fk_b[hh][lo:hi], NEG_INF)
        mt = _colmax(t_own)
        t_far = None
        if i > 0:
            t_far = _dot_nt(k_ref[0, 0:lo, :], qh) - fk_b[hh][0:lo]
            mt = jnp.maximum(mt, _colmax(t_far))
        return t_own, t_far, mt, fq

    def weighted_values(i, hh, state):
        lo, hi = i * tq, (i + 1) * tq
        t_own, t_far, mt, fq = state
        m = mt + fq
        shift = m - fq
        p_own = jnp.exp2(t_own - shift)
        l = _colsum(p_own)
        acc = _dot(vt_ref[0, :, lo:hi], p_own.astype(BF16))
        if i > 0:
            p_far = jnp.exp2(t_far - shift)
            l = l + _colsum(p_far)
            acc = acc + _dot(vt_ref[0, :, 0:lo], p_far.astype(BF16))
        return acc * (1.0 / l)

    res = _pipelined(s_len // tq, scores, weighted_values)
    for i in range(s_len // tq):
        out_t = jnp.where(orow < HEAD_DIM, res[(i, 0)], res[(i, 1)])
        o_ref[0, i * tq:(i + 1) * tq, :] = out_t.T.astype(o_ref.dtype)


def _fox(q, k, vt, fcol, frow):
    bsz, s, w = q.shape
    npair = w // PAIR
    return pl.pallas_call(
        _fox_kernel,
        grid=(bsz, npair),
        in_specs=[
            pl.BlockSpec((1, s, PAIR), lambda b, j: (b, 0, j)),
            pl.BlockSpec((1, s, PAIR), lambda b, j: (b, 0, j)),
            pl.BlockSpec((1, PAIR, s), lambda b, j: (b, j, 0)),
            pl.BlockSpec((1, 1, s, 2), lambda b, j: (b, j, 0, 0)),
            pl.BlockSpec((1, 1, 2, s), lambda b, j: (b, j, 0, 0)),
        ],
        out_specs=pl.BlockSpec((1, s, PAIR), lambda b, j: (b, 0, j)),
        out_shape=jax.ShapeDtypeStruct((bsz, s, w), BF16),
        compiler_params=_params(("parallel", "parallel")),
        name="fox",
    )(q, k, vt, fcol, frow)


def _moba_kernel(far_ref, q_ref, k_ref, vt_ref, kmean_ref, tbl_ref, o_ref):
    j = pl.program_id(1)
    s_len = q_ref.shape[1]
    tq = KV_BLOCK
    nb = kmean_ref.shape[1]
    heads = _head_masks(tq)
    kheads = _head_masks(nb)
    blk = lax.broadcasted_iota(jnp.int32, (nb, tq), 0)
    orow = lax.broadcasted_iota(jnp.int32, (PAIR, tq), 0)
    kmean = kmean_ref[0]
    km_parts = [_split3(jnp.where(kheads[hh], kmean, 0.0)) for hh in range(2)]

    def scores(i, hh):
        lo, hi = i * tq, (i + 1) * tq
        qf = q_ref[0, lo:hi, :].astype(F32)
        qh = jnp.where(heads[hh], qf * (LOG2E * HEAD_DIM ** -0.5), 0.0).astype(BF16)
        far = far_ref[2 * j + hh] * LOG2E
        s_own = _dot_nt(k_ref[0, lo:hi, :], qh) + tbl_ref[hh, 0]
        mt = _colmax(s_own)
        s_prev = b_prev = s_far = None
        biases = []
        if i > 0:
            q_gate = jnp.where(heads[hh], qf, 0.0).astype(BF16)
            km_hi, km_mid, km_lo = km_parts[hh]
            g = _dot_nt(km_lo, q_gate) + _dot_nt(km_mid, q_gate) + _dot_nt(km_hi, q_gate)
            valid = blk < i
            g = jnp.where(valid, g, NEG_INF)
            rank = jnp.zeros((nb, tq), jnp.int32)
            for mm in range(i):
                gm = g[mm:mm + 1, :]
                beats = (gm > g) | ((gm == g) & (blk > mm))
                rank = rank + beats.astype(jnp.int32)
            sel_bias = jnp.where(valid & (rank < MOBA_TOPK), 0.0, NEG_INF)
            s_prev = _dot_nt(k_ref[0, lo - tq:lo, :], qh) + tbl_ref[hh, 1]
            b_prev = sel_bias[i - 1:i, :]
            mt = jnp.maximum(mt, _colmax(s_prev) + b_prev)
        if i > 1:
            s_far = _dot_nt(k_ref[0, 0:lo - tq, :], qh)
            for n in range(i - 1):
                b_n = sel_bias[n:n + 1, :] + far
                biases.append(b_n)
                mt = jnp.maximum(mt, _colmax(s_far[n * tq:(n + 1) * tq]) + b_n)
        return s_own, s_prev, b_prev, s_far, biases, mt

    def weighted_values(i, hh, state):
        lo, hi = i * tq, (i + 1) * tq
        s_own, s_prev, b_prev, s_far, biases, mt = state
        p = jnp.exp2(s_own - mt)
        l = _colsum(p)
        acc = _dot(vt_ref[0, :, lo:hi], p.astype(BF16))
        if i > 0:
            p = jnp.exp2(s_prev - (mt - b_prev))
            l = l + _colsum(p)
            acc = acc + _dot(vt_ref[0, :, lo - tq:lo], p.astype(BF16))
        if i > 1:
            for n in range(i - 1):
                p = jnp.exp2(s_far[n * tq:(n + 1) * tq] - (mt - biases[n]))
                l = l + _colsum(p)
                acc = acc + _dot(vt_ref[0, :, n * tq:(n + 1) * tq], p.astype(BF16))
        return acc * (1.0 / l)

    res = _pipelined(s_len // tq, scores, weighted_values)
    for i in range(s_len // tq):
        out_t = jnp.where(orow < HEAD_DIM, res[(i, 0)], res[(i, 1)])
        o_ref[0, i * tq:(i + 1) * tq, :] = out_t.T.astype(o_ref.dtype)


def _moba(far, q, k, vt, kmean, tbl):
    bsz, s, w = q.shape
    npair = w // PAIR
    nb = s // KV_BLOCK
    return pl.pallas_call(
        _moba_kernel,
        grid=(bsz, npair),
        in_specs=[
            pl.BlockSpec(memory_space=pltpu.SMEM),
            pl.BlockSpec((1, s, PAIR), lambda b, j: (b, 0, j)),
            pl.BlockSpec((1, s, PAIR), lambda b, j: (b, 0, j)),
            pl.BlockSpec((1, PAIR, s), lambda b, j: (b, j, 0)),
            pl.BlockSpec((1, nb, PAIR), lambda b, j: (b, 0, j)),
            pl.BlockSpec((2, 2, KV_BLOCK, KV_BLOCK), lambda b, j: (j, 0, 0, 0)),
        ],
        out_specs=pl.BlockSpec((1, s, PAIR), lambda b, j: (b, 0, j)),
        out_shape=jax.ShapeDtypeStruct((bsz, s, w), BF16),
        compiler_params=_params(("parallel", "parallel")),
        name="moba",
    )(far, q, k, vt, kmean, tbl)


def _mix_ffn_kernel(fox_ref, moba_ref, x_ref, wof_ref, wom_ref, gate1_ref, shift_ref, scale_ref, g_ref,
                    wg_ref, wu_ref, wd_ref, gate2_ref, o_ref):
    mix = _dot(fox_ref[0], wof_ref[...]) + _dot(moba_ref[0], wom_ref[...])
    x1 = x_ref[0] + gate1_ref[0] * mix
    ms = jnp.mean(x1 * x1, axis=-1, keepdims=True)
    y = x1 * lax.rsqrt(ms + EPS) * g_ref[...]
    h2 = (y * (1.0 + scale_ref[0]) + shift_ref[0]).astype(BF16)
    gt = _dot(h2, wg_ref[...])
    up = _dot(h2, wu_ref[...])
    act = (gt * jax.nn.sigmoid(gt) * up).astype(BF16)
    o_ref[0] = x1 + gate2_ref[0] * _dot(act, wd_ref[...])


def _mix_ffn(fox, moba, x, wo_f, wo_m, gate1, shift, scale, g, wg, wu, wd, gate2):
    bsz, s, d = x.shape
    dff = wg.shape[1]
    tm = ROW_TILE
    wf = fox.shape[2]
    wm = moba.shape[2]
    row = lambda b, t: (b, t, 0)
    per_b = lambda b, t: (b, 0, 0)
    return pl.pallas_call(
        _mix_ffn_kernel,
        grid=(bsz, s // tm),
        in_specs=[
            pl.BlockSpec((1, tm, wf), row),
            pl.BlockSpec((1, tm, wm), row),
            pl.BlockSpec((1, tm, d), row),
            _const_spec((wf, d)),
            _const_spec((wm, d)),
            pl.BlockSpec((1, 1, d), per_b),
            pl.BlockSpec((1, 1, d), per_b),
            pl.BlockSpec((1, 1, d), per_b),
            _const_spec((1, d)),
            _const_spec((d, dff)),
            _const_spec((d, dff)),
            _const_spec((dff, d)),
            pl.BlockSpec((1, 1, d), per_b),
        ],
        out_specs=pl.BlockSpec((1, tm, d), row),
        out_shape=jax.ShapeDtypeStruct((bsz, s, d), F32),
        compiler_params=_params(("parallel", "parallel")),
        name="mix_ffn",
    )(fox, moba, x, wo_f, wo_m, gate1, shift, scale, g, wg, wu, wd, gate2)


def kernel(x, c, w_ada, b_ada, norm1, norm2, w_in, b_forget, q_norm_fox, k_norm_fox,
           q_norm_moba, k_norm_moba, rel_bias, w_o, w_gate, w_up, w_down):
    bsz, s, d = x.shape
    depth = w_ada.shape[0]
    hf = b_forget.shape[1]
    hm = rel_bias.shape[1]
    n_buckets = rel_bias.shape[0]
    wf = hf * HEAD_DIM
    wm = hm * HEAD_DIM
    nb = s // KV_BLOCK
    assert q_norm_fox.shape[1] == HEAD_DIM and s % KV_BLOCK == 0 and s % FOX_TQ == 0
    assert wf % 256 == 0 and wm == wf and hf <= 128 and ROW_TILE == KV_BLOCK
    assert w_in.shape[2] == 3 * wf + hf + 3 * wm

    ind = jnp.asarray(np.kron(np.eye(256 // HEAD_DIM), np.ones((HEAD_DIM, HEAD_DIM))), BF16)
    tri = jnp.asarray(np.triu(np.ones((KV_BLOCK, KV_BLOCK))), BF16)
    rows = hf * nb
    rr = np.arange(rows)
    blk_lower = ((rr[:, None] // nb == rr[None, :] // nb) & (rr[None, :] % nb < rr[:, None] % nb))
    blk_lower = jnp.asarray(blk_lower.astype(np.float32), BF16)
    kl = np.arange(KV_BLOCK)[:, None]
    ql = np.arange(KV_BLOCK)[None, :]
    bkt = jnp.asarray(np.stack([_t5_bucket_np(ql - kl, n_buckets),
                                _t5_bucket_np(KV_BLOCK + ql - kl, n_buckets)]))
    assert KV_BLOCK + 1 >= MAX_DISTANCE

    tables = _t5_tables(rel_bias.T, bkt)
    far = rel_bias[n_buckets - 1, :]

    for l in range(depth):
        mod = _adaln(c, w_ada[l], b_ada[l]).reshape(bsz, 6, 1, d)
        shift1, scale1, gate1, shift2, scale2, gate2 = [mod[:, t] for t in range(6)]

        wl = w_in[l]
        o_fv, o_ff, o_mq = 2 * wf, 3 * wf, 3 * wf + hf
        w_all = jnp.concatenate(
            [wl[:, :o_fv], wl[:, o_mq:o_mq + 2 * wm], wl[:, o_ff:o_ff + hf], jnp.zeros((d, 128 - hf), wl.dtype)],
            axis=1).astype(BF16)
        wvt = jnp.concatenate([wl[:, o_fv:o_ff], wl[:, o_mq + 2 * wm:]], axis=1).T.astype(BF16)
        gains = jnp.stack([jnp.tile(q_norm_fox[l], hf), jnp.tile(k_norm_fox[l], hf),
                           jnp.tile(q_norm_moba[l], hm), jnp.tile(k_norm_moba[l], hm)])
        fq, fk, mq, mk, fvt, mvt, ff, kmean = _in_proj(
            x, shift1, scale1, norm1[l].reshape(1, d), w_all, wvt, ind, gains, wf, wm, hf)

        ff_rows = jnp.transpose(ff, (0, 2, 1)).reshape(bsz, rows, KV_BLOCK)
        b_rows = jnp.repeat(b_forget[l], nb).reshape(rows, 1)
        frow = _forget(ff_rows, b_rows, tri, blk_lower).reshape(bsz, hf // 2, 2, s)
        fcol = jnp.transpose(frow, (0, 1, 3, 2))
        fox_out = _fox(fq, fk, fvt, fcol, frow)
        moba_out = _moba(far, mq, mk, mvt, kmean, tables)

        wo = w_o[l].astype(BF16)
        x = _mix_ffn(fox_out, moba_out, x, wo[:wf], wo[wf:], gate1, shift2, scale2, norm2[l].reshape(1, d),
                     w_gate[l].astype(BF16), w_up[l].astype(BF16), w_down[l].astype(BF16), gate2)
    return x
```

```python
import functools
import math

import numpy as np
import jax
import jax.numpy as jnp
from jax import lax
from jax.experimental import pallas as pl
from jax.experimental.pallas import tpu as pltpu

F32 = jnp.float32
BF16 = jnp.bfloat16

EPS = 1e-6
HEAD_DIM = 64
PAIR = 2 * HEAD_DIM
KV_BLOCK = 256
MOBA_TOPK = 3
MAX_DISTANCE = 128
ROW_TILE = 512
VMEM_LIMIT = 56 * 1024 * 1024
NEG_INF = float("-inf")
LOG2E = 1.4426950408889634
FOX_TQ = 512


def _split3(x):
    hi = x.astype(BF16)
    r = x - hi.astype(F32)
    mid = r.astype(BF16)
    lo = (r - mid.astype(F32)).astype(BF16)
    return hi, mid, lo


def _dot(a, b):
    return jnp.dot(a, b, preferred_element_type=F32)


def _dot_nt(a, b):
    return lax.dot_general(a, b, (((1,), (1,)), ((), ())), preferred_element_type=F32)


def _params(sem):
    return pltpu.CompilerParams(dimension_semantics=sem, vmem_limit_bytes=VMEM_LIMIT)


def _const_spec(shape):
    nd = len(shape)
    return pl.BlockSpec(shape, lambda *_: (0,) * nd, pipeline_mode=pl.Buffered(1))


def _adaln_kernel(c_ref, w_ref, b_ref, o_ref):
    c = c_ref[...]
    a = c * jax.nn.sigmoid(c)
    w = w_ref[...]
    a_hi, a_mid, a_lo = _split3(a)
    w_hi, w_mid, w_lo = _split3(w)
    acc = _dot(a_lo, w_hi) + _dot(a_mid, w_mid) + _dot(a_hi, w_lo)
    acc = acc + _dot(a_mid, w_hi) + _dot(a_hi, w_mid)
    acc = acc + _dot(a_hi, w_hi)
    o_ref[...] = acc + b_ref[...]


def _adaln(c, w, b):
    bsz, d = c.shape
    n = w.shape[1]
    tn = 1024 if n % 1024 == 0 else n
    return pl.pallas_call(
        _adaln_kernel,
        grid=(n // tn,),
        in_specs=[
            pl.BlockSpec((bsz, d), lambda i: (0, 0)),
            pl.BlockSpec((d, tn), lambda i: (0, i)),
            pl.BlockSpec((1, tn), lambda i: (0, i)),
        ],
        out_specs=pl.BlockSpec((bsz, tn), lambda i: (0, i)),
        out_shape=jax.ShapeDtypeStruct((bsz, n), F32),
        compiler_params=_params(("parallel",)),
        name="adaln",
    )(c, w, b.reshape(1, n))


def _head_rms(y, ind, gain):
    cols = []
    for c0 in range(0, y.shape[1], 256):
        yc = y[:, c0:c0 + 256]
        ss = _dot((yc * yc).astype(BF16), ind)
        cols.append(yc * lax.rsqrt(ss * (1.0 / HEAD_DIM) + EPS) * gain[:, c0:c0 + 256])
    return jnp.concatenate(cols, axis=1)


def _in_proj_kernel(x_ref, shift_ref, scale_ref, g_ref, w_ref, wvt_ref, ind_ref, gains_ref,
                    fq_ref, fk_ref, mq_ref, mk_ref, fvt_ref, mvt_ref, ff_ref, kmean_ref, *, wf, wm):
    t = pl.program_id(1)
    x = x_ref[0]
    ms = jnp.mean(x * x, axis=-1, keepdims=True)
    y = x * lax.rsqrt(ms + EPS) * g_ref[...]
    h = (y * (1.0 + scale_ref[0]) + shift_ref[0]).astype(BF16)
    proj = _dot(h, w_ref[...])
    vt = _dot_nt(wvt_ref[...], h)
    ind = ind_ref[...]
    gains = gains_ref[...]
    o = 0
    fq = _head_rms(proj[:, o:o + wf], ind, gains[0:1]); o += wf
    fk = _head_rms(proj[:, o:o + wf], ind, gains[1:2]); o += wf
    mq = _head_rms(proj[:, o:o + wm], ind, gains[2:3]); o += wm
    mk = _head_rms(proj[:, o:o + wm], ind, gains[3:4]); o += wm
    ff = proj[:, o:o + ff_ref.shape[2]]
    fq_ref[0] = fq.astype(BF16)
    fk_ref[0] = fk.astype(BF16)
    mq_ref[0] = mq.astype(BF16)
    mk_ref[0] = mk.astype(BF16)
    fvt_ref[0] = vt[:wf].astype(BF16)
    mvt_ref[0] = vt[wf:].astype(BF16)
    ff_ref[0] = ff
    per_tile = mk.shape[0] // KV_BLOCK
    for r in range(per_tile):
        blk_sum = jnp.sum(mk[r * KV_BLOCK:(r + 1) * KV_BLOCK], axis=0, keepdims=True)
        kmean_ref[0, pl.ds(t * per_tile + r, 1), :] = blk_sum * (1.0 / KV_BLOCK)


def _in_proj(x, shift, scale, g, w_all, wvt, ind, gains, wf, wm, hf):
    bsz, s, d = x.shape
    tm = ROW_TILE
    nb = s // KV_BLOCK
    row = lambda b, t: (b, t, 0)
    col = lambda b, t: (b, 0, t)
    per_b = lambda b, t: (b, 0, 0)
    return pl.pallas_call(
        functools.partial(_in_proj_kernel, wf=wf, wm=wm),
        grid=(bsz, s // tm),
        in_specs=[
            pl.BlockSpec((1, tm, d), row),
            pl.BlockSpec((1, 1, d), per_b),
            pl.BlockSpec((1, 1, d), per_b),
            _const_spec((1, d)),
            _const_spec(w_all.shape),
            _const_spec(wvt.shape),
            _const_spec((256, 256)),
            _const_spec(gains.shape),
        ],
        out_specs=[
            pl.BlockSpec((1, tm, wf), row),
            pl.BlockSpec((1, tm, wf), row),
            pl.BlockSpec((1, tm, wm), row),
            pl.BlockSpec((1, tm, wm), row),
            pl.BlockSpec((1, wf, tm), col),
            pl.BlockSpec((1, wm, tm), col),
            pl.BlockSpec((1, tm, hf), row),
            pl.BlockSpec((1, nb, wm), per_b),
        ],
        out_shape=[
            jax.ShapeDtypeStruct((bsz, s, wf), BF16),
            jax.ShapeDtypeStruct((bsz, s, wf), BF16),
            jax.ShapeDtypeStruct((bsz, s, wm), BF16),
            jax.ShapeDtypeStruct((bsz, s, wm), BF16),
            jax.ShapeDtypeStruct((bsz, wf, s), BF16),
            jax.ShapeDtypeStruct((bsz, wm, s), BF16),
            jax.ShapeDtypeStruct((bsz, s, hf), F32),
            jax.ShapeDtypeStruct((bsz, nb, wm), F32),
        ],
        compiler_params=_params(("parallel", "arbitrary")),
        name="in_proj",
    )(x, shift, scale, g, w_all, wvt, ind, gains)


def _forget_kernel(ff_ref, b_ref, tri_ref, blk_ref, o_ref):
    z = ff_ref[0] + b_ref[...]
    lf = jnp.minimum(z, 0.0) - jnp.log1p(jnp.exp(-jnp.abs(z)))
    tri = tri_ref[...]
    hi, mid, lo = _split3(lf)
    local = _dot(lo, tri) + _dot(mid, tri) + _dot(hi, tri)
    tot = jnp.broadcast_to(local[:, KV_BLOCK - 1:KV_BLOCK], (local.shape[0], 128))
    blk = blk_ref[...]
    t_hi, t_mid, t_lo = _split3(tot)
    off = _dot(blk, t_lo) + _dot(blk, t_mid) + _dot(blk, t_hi)
    o_ref[0] = local + off[:, 0:1]


def _forget(ff_rows, b_rows, tri, blk):
    bsz, rows, _ = ff_rows.shape
    return pl.pallas_call(
        _forget_kernel,
        grid=(bsz,),
        in_specs=[
            pl.BlockSpec((1, rows, KV_BLOCK), lambda b: (b, 0, 0)),
            pl.BlockSpec((rows, 1), lambda b: (0, 0)),
            pl.BlockSpec((KV_BLOCK, KV_BLOCK), lambda b: (0, 0)),
            pl.BlockSpec((rows, rows), lambda b: (0, 0)),
        ],
        out_specs=pl.BlockSpec((1, rows, KV_BLOCK), lambda b: (b, 0, 0)),
        out_shape=jax.ShapeDtypeStruct((bsz, rows, KV_BLOCK), F32),
        compiler_params=_params(("parallel",)),
        name="forget",
    )(ff_rows, b_rows, tri, blk)


def _t5_bucket_np(dist, n_buckets):
    n = np.maximum(dist, 0)
    max_exact = n_buckets // 2
    nf = np.maximum(n, 1).astype(np.float32)
    large = max_exact + (np.log(nf / np.float32(max_exact)) / np.float32(math.log(MAX_DISTANCE / max_exact))
                         * np.float32(n_buckets - max_exact)).astype(np.int32)
    large = np.minimum(large, n_buckets - 1)
    return np.where(n < max_exact, n, large).astype(np.int32)


def _t5_kernel(relb_ref, bkt_ref, o_ref, *, n_buckets):
    h = pl.program_id(0)
    kl = lax.broadcasted_iota(jnp.int32, (KV_BLOCK, KV_BLOCK), 0)
    ql = lax.broadcasted_iota(jnp.int32, (KV_BLOCK, KV_BLOCK), 1)
    for t in range(2):
        b = bkt_ref[t]
        acc = jnp.zeros((KV_BLOCK, KV_BLOCK), F32)
        for u in range(n_buckets):
            acc = jnp.where(b == u, relb_ref[h, u], acc)
        if t == 0:
            acc = jnp.where(kl <= ql, acc, NEG_INF)
        o_ref[0, t] = acc * LOG2E


def _t5_tables(rel_bias_t, bkt):
    nh, n_buckets = rel_bias_t.shape
    return pl.pallas_call(
        functools.partial(_t5_kernel, n_buckets=n_buckets),
        grid=(nh,),
        in_specs=[
            pl.BlockSpec(memory_space=pltpu.SMEM),
            pl.BlockSpec((2, KV_BLOCK, KV_BLOCK), lambda h: (0, 0, 0)),
        ],
        out_specs=pl.BlockSpec((1, 2, KV_BLOCK, KV_BLOCK), lambda h: (h, 0, 0, 0)),
        out_shape=jax.ShapeDtypeStruct((nh, 2, KV_BLOCK, KV_BLOCK), F32),
        compiler_params=_params(("parallel",)),
        name="t5_tables",
    )(rel_bias_t, bkt)


def _colmax(x):
    return jnp.max(x, axis=0, keepdims=True)


def _colsum(x):
    return jnp.sum(x, axis=0, keepdims=True)


def _head_masks(rows):
    lane = lax.broadcasted_iota(jnp.int32, (rows, PAIR), 1)
    return [lane < HEAD_DIM, lane >= HEAD_DIM]


def _pipelined(n_tiles, first_stage, second_stage, depth):
    units = [(i, hh) for i in range(n_tiles) for hh in range(2)]
    res = {}
    pending = []
    for u in units:
        pending.append((u, first_stage(*u)))
        if len(pending) > depth:
            done, state = pending.pop(0)
            res[done] = second_stage(*done, state)
    for done, state in pending:
        res[done] = second_stage(*done, state)
    return res


def _fox_kernel(q_ref, k_ref, vt_ref, fchunk_ref, frow_ref, o_ref):
    s_len = q_ref.shape[1]
    tq = FOX_TQ
    heads = _head_masks(tq)
    orow = lax.broadcasted_iota(jnp.int32, (PAIR, tq), 0)
    kl = lax.broadcasted_iota(jnp.int32, (tq, tq), 0)
    ql = lax.broadcasted_iota(jnp.int32, (tq, tq), 1)
    causal = kl <= ql
    nb = s_len // KV_BLOCK
    fchunk = fchunk_ref[0, 0] * LOG2E
    fk_b = [jnp.concatenate([jnp.broadcast_to(fchunk[:, hh * nb + c:hh * nb + c + 1], (KV_BLOCK, tq))
                             for c in range(nb)], axis=0) for hh in range(2)]

    def scores(i, hh):
        lo, hi = i * tq, (i + 1) * tq
        qf = q_ref[0, lo:hi, :].astype(F32) * (LOG2E * HEAD_DIM ** -0.5)
        qh = jnp.where(heads[hh], qf, 0.0).astype(BF16)
        fq = frow_ref[0, 0, hh:hh + 1, lo:hi] * LOG2E
        t_own = jnp.where(causal, _dot_nt(k_ref[0, lo:hi, :], qh) - fk_b[hh][lo:hi], NEG_INF)
        mt = _colmax(t_own)
        t_far = None
        if i > 0:
            t_far = _dot_nt(k_ref[0, 0:lo, :], qh) - fk_b[hh][0:lo]
            mt = jnp.maximum(mt, _colmax(t_far))
        return t_own, t_far, mt, fq

    def weighted_values(i, hh, state):
        lo, hi = i * tq, (i + 1) * tq
        t_own, t_far, mt, fq = state
        m = mt + fq
        shift = m - fq
        p_own = jnp.exp2(t_own - shift)
        l = _colsum(p_own)
        acc = _dot(vt_ref[0, :, lo:hi], p_own.astype(BF16))
        if i > 0:
            p_far = jnp.exp2(t_far - shift)
            l = l + _colsum(p_far)
            acc = acc + _dot(vt_ref[0, :, 0:lo], p_far.astype(BF16))
        return acc * (1.0 / l)

    res = _pipelined(s_len // tq, scores, weighted_values, depth=1)
    for i in range(s_len // tq):
        out_t = jnp.where(orow < HEAD_DIM, res[(i, 0)], res[(i, 1)])
        o_ref[0, i * tq:(i + 1) * tq, :] = out_t.T.astype(o_ref.dtype)


def _fox(q, k, vt, fchunk, frow):
    bsz, s, w = q.shape
    npair = w // PAIR
    nb = s // KV_BLOCK
    return pl.pallas_call(
        _fox_kernel,
        grid=(bsz, npair),
        in_specs=[
            pl.BlockSpec((1, s, PAIR), lambda b, j: (b, 0, j)),
            pl.BlockSpec((1, s, PAIR), lambda b, j: (b, 0, j)),
            pl.BlockSpec((1, PAIR, s), lambda b, j: (b, j, 0)),
            pl.BlockSpec((1, 1, KV_BLOCK, 2 * nb), lambda b, j: (b, j, 0, 0)),
            pl.BlockSpec((1, 1, 2, s), lambda b, j: (b, j, 0, 0)),
        ],
        out_specs=pl.BlockSpec((1, s, PAIR), lambda b, j: (b, 0, j)),
        out_shape=jax.ShapeDtypeStruct((bsz, s, w), BF16),
        compiler_params=_params(("parallel", "parallel")),
        name="fox",
    )(q, k, vt, fchunk, frow)


def _moba_kernel(far_ref, q_ref, k_ref, vt_ref, kmean_ref, tbl_ref, o_ref):
    j = pl.program_id(1)
    s_len = q_ref.shape[1]
    tq = KV_BLOCK
    nb = kmean_ref.shape[1]
    heads = _head_masks(tq)
    kheads = _head_masks(nb)
    blk = lax.broadcasted_iota(jnp.int32, (nb, tq), 0)
    orow = lax.broadcasted_iota(jnp.int32, (PAIR, tq), 0)
    kmean = kmean_ref[0]
    km_parts = [_split3(jnp.where(kheads[hh], kmean, 0.0)) for hh in range(2)]

    def scores(i, hh):
        lo, hi = i * tq, (i + 1) * tq
        qf = q_ref[0, lo:hi, :].astype(F32)
        qh = jnp.where(heads[hh], qf * (LOG2E * HEAD_DIM ** -0.5), 0.0).astype(BF16)
        far = far_ref[2 * j + hh] * LOG2E
        s_own = _dot_nt(k_ref[0, lo:hi, :], qh) + tbl_ref[hh, 0]
        mt = _colmax(s_own)
        s_prev = b_prev = s_far = None
        biases = []
        if i > 0:
            q_gate = jnp.where(heads[hh], qf, 0.0).astype(BF16)
            km_hi, km_mid, km_lo = km_parts[hh]
            g = _dot_nt(km_lo, q_gate) + _dot_nt(km_mid, q_gate) + _dot_nt(km_hi, q_gate)
            valid = blk < i
            g = jnp.where(valid, g, NEG_INF)
            rank = jnp.zeros((nb, tq), jnp.int32)
            for mm in range(i):
                gm = g[mm:mm + 1, :]
                beats = (gm > g) | ((gm == g) & (blk > mm))
                rank = rank + beats.astype(jnp.int32)
            sel_bias = jnp.where(valid & (rank < MOBA_TOPK), 0.0, NEG_INF)
            s_prev = _dot_nt(k_ref[0, lo - tq:lo, :], qh) + tbl_ref[hh, 1]
            b_prev = sel_bias[i - 1:i, :]
            mt = jnp.maximum(mt, _colmax(s_prev) + b_prev)
        if i > 1:
            s_far = _dot_nt(k_ref[0, 0:lo - tq, :], qh)
            for n in range(i - 1):
                b_n = sel_bias[n:n + 1, :] + far
                biases.append(b_n)
                mt = jnp.maximum(mt, _colmax(s_far[n * tq:(n + 1) * tq]) + b_n)
        return s_own, s_prev, b_prev, s_far, biases, mt

    def weighted_values(i, hh, state):
        lo, hi = i * tq, (i + 1) * tq
        s_own, s_prev, b_prev, s_far, biases, mt = state
        p = jnp.exp2(s_own - mt)
        l = _colsum(p)
        acc = _dot(vt_ref[0, :, lo:hi], p.astype(BF16))
        if i > 0:
            p = jnp.exp2(s_prev - (mt - b_prev))
            l = l + _colsum(p)
            acc = acc + _dot(vt_ref[0, :, lo - tq:lo], p.astype(BF16))
        if i > 1:
            for n in range(i - 1):
                p = jnp.exp2(s_far[n * tq:(n + 1) * tq] - (mt - biases[n]))
                l = l + _colsum(p)
                acc = acc + _dot(vt_ref[0, :, n * tq:(n + 1) * tq], p.astype(BF16))
        return acc * (1.0 / l)

    res = _pipelined(s_len // tq, scores, weighted_values, depth=2)
    for i in range(s_len // tq):
        out_t = jnp.where(orow < HEAD_DIM, res[(i, 0)], res[(i, 1)])
        o_ref[0, i * tq:(i + 1) * tq, :] = out_t.T.astype(o_ref.dtype)


def _moba(far, q, k, vt, kmean, tbl):
    bsz, s, w = q.shape
    npair = w // PAIR
    nb = s // KV_BLOCK
    return pl.pallas_call(
        _moba_kernel,
        grid=(bsz, npair),
        in_specs=[
            pl.BlockSpec(memory_space=pltpu.SMEM),
            pl.BlockSpec((1, s, PAIR), lambda b, j: (b, 0, j)),
            pl.BlockSpec((1, s, PAIR), lambda b, j: (b, 0, j)),
            pl.BlockSpec((1, PAIR, s), lambda b, j: (b, j, 0)),
            pl.BlockSpec((1, nb, PAIR), lambda b, j: (b, 0, j)),
            pl.BlockSpec((2, 2, KV_BLOCK, KV_BLOCK), lambda b, j: (j, 0, 0, 0)),
        ],
        out_specs=pl.BlockSpec((1, s, PAIR), lambda b, j: (b, 0, j)),
        out_shape=jax.ShapeDtypeStruct((bsz, s, w), BF16),
        compiler_params=_params(("parallel", "parallel")),
        name="moba",
    )(far, q, k, vt, kmean, tbl)


def _mix_ffn_kernel(fox_ref, moba_ref, x_ref, wof_ref, wom_ref, gate1_ref, shift_ref, scale_ref, g_ref,
                    wg_ref, wu_ref, wd_ref, gate2_ref, o_ref):
    mix = _dot(fox_ref[0], wof_ref[...]) + _dot(moba_ref[0], wom_ref[...])
    x1 = x_ref[0] + gate1_ref[0] * mix
    ms = jnp.mean(x1 * x1, axis=-1, keepdims=True)
    y = x1 * lax.rsqrt(ms + EPS) * g_ref[...]
    h2 = (y * (1.0 + scale_ref[0]) + shift_ref[0]).astype(BF16)
    gt = _dot(h2, wg_ref[...])
    up = _dot(h2, wu_ref[...])
    act = (gt * jax.nn.sigmoid(gt) * up).astype(BF16)
    o_ref[0] = x1 + gate2_ref[0] * _dot(act, wd_ref[...])


def _mix_ffn(fox, moba, x, wo_f, wo_m, gate1, shift, scale, g, wg, wu, wd, gate2):
    bsz, s, d = x.shape
    dff = wg.shape[1]
    tm = ROW_TILE
    wf = fox.shape[2]
    wm = moba.shape[2]
    row = lambda b, t: (b, t, 0)
    per_b = lambda b, t: (b, 0, 0)
    return pl.pallas_call(
        _mix_ffn_kernel,
        grid=(bsz, s // tm),
        in_specs=[
            pl.BlockSpec((1, tm, wf), row),
            pl.BlockSpec((1, tm, wm), row),
            pl.BlockSpec((1, tm, d), row),
            _const_spec((wf, d)),
            _const_spec((wm, d)),
            pl.BlockSpec((1, 1, d), per_b),
            pl.BlockSpec((1, 1, d), per_b),
            pl.BlockSpec((1, 1, d), per_b),
            _const_spec((1, d)),
            _const_spec((d, dff)),
            _const_spec((d, dff)),
            _const_spec((dff, d)),
            pl.BlockSpec((1, 1, d), per_b),
        ],
        out_specs=pl.BlockSpec((1, tm, d), row),
        out_shape=jax.ShapeDtypeStruct((bsz, s, d), F32),
        compiler_params=_params(("parallel", "parallel")),
        name="mix_ffn",
    )(fox, moba, x, wo_f, wo_m, gate1, shift, scale, g, wg, wu, wd, gate2)


def kernel(x, c, w_ada, b_ada, norm1, norm2, w_in, b_forget, q_norm_fox, k_norm_fox,
           q_norm_moba, k_norm_moba, rel_bias, w_o, w_gate, w_up, w_down):
    bsz, s, d = x.shape
    depth = w_ada.shape[0]
    hf = b_forget.shape[1]
    hm = rel_bias.shape[1]
    n_buckets = rel_bias.shape[0]
    wf = hf * HEAD_DIM
    wm = hm * HEAD_DIM
    nb = s // KV_BLOCK
    assert q_norm_fox.shape[1] == HEAD_DIM and s % KV_BLOCK == 0 and s % FOX_TQ == 0
    assert wf % 256 == 0 and wm == wf and hf <= 128 and ROW_TILE % KV_BLOCK == 0 and s % ROW_TILE == 0
    assert w_in.shape[2] == 3 * wf + hf + 3 * wm

    ind = jnp.asarray(np.kron(np.eye(256 // HEAD_DIM), np.ones((HEAD_DIM, HEAD_DIM))), BF16)
    tri = jnp.asarray(np.triu(np.ones((KV_BLOCK, KV_BLOCK))), BF16)
    rows = hf * nb
    rr = np.arange(rows)
    blk_lower = ((rr[:, None] // nb == rr[None, :] // nb) & (rr[None, :] % nb < rr[:, None] % nb))
    blk_lower = jnp.asarray(blk_lower.astype(np.float32), BF16)
    kl = np.arange(KV_BLOCK)[:, None]
    ql = np.arange(KV_BLOCK)[None, :]
    bkt = jnp.asarray(np.stack([_t5_bucket_np(ql - kl, n_buckets),
                                _t5_bucket_np(KV_BLOCK + ql - kl, n_buckets)]))
    assert KV_BLOCK + 1 >= MAX_DISTANCE

    tables = _t5_tables(rel_bias.T, bkt)
    far = rel_bias[n_buckets - 1, :]

    for l in range(depth):
        mod = _adaln(c, w_ada[l], b_ada[l]).reshape(bsz, 6, 1, d)
        shift1, scale1, gate1, shift2, scale2, gate2 = [mod[:, t] for t in range(6)]

        wl = w_in[l]
        o_fv, o_ff, o_mq = 2 * wf, 3 * wf, 3 * wf + hf
        w_all = jnp.concatenate(
            [wl[:, :o_fv], wl[:, o_mq:o_mq + 2 * wm], wl[:, o_ff:o_ff + hf], jnp.zeros((d, 128 - hf), wl.dtype)],
            axis=1).astype(BF16)
        wvt = jnp.concatenate([wl[:, o_fv:o_ff], wl[:, o_mq + 2 * wm:]], axis=1).T.astype(BF16)
        gains = jnp.stack([jnp.tile(q_norm_fox[l], hf), jnp.tile(k_norm_fox[l], hf),
                           jnp.tile(q_norm_moba[l], hm), jnp.tile(k_norm_moba[l], hm)])
        fq, fk, mq, mk, fvt, mvt, ff, kmean = _in_proj(
            x, shift1, scale1, norm1[l].reshape(1, d), w_all, wvt, ind, gains, wf, wm, hf)

        ff_rows = jnp.transpose(ff, (0, 2, 1)).reshape(bsz, rows, KV_BLOCK)
        b_rows = jnp.repeat(b_forget[l], nb).reshape(rows, 1)
        frow = _forget(ff_rows, b_rows, tri, blk_lower).reshape(bsz, hf // 2, 2, s)
        fchunk = jnp.transpose(frow.reshape(bsz, hf // 2, 2, nb, KV_BLOCK), (0, 1, 4, 2, 3))
        fchunk = fchunk.reshape(bsz, hf // 2, KV_BLOCK, 2 * nb)
        fox_out = _fox(fq, fk, fvt, fchunk, frow)
        moba_out = _moba(far, mq, mk, mvt, kmean, tables)

        wo = w_o[l].astype(BF16)
        x = _mix_ffn(fox_out, moba_out, x, wo[:wf], wo[wf:], gate1, shift2, scale2, norm2[l].reshape(1, d),
                     w_gate[l].astype(BF16), w_up[l].astype(BF16), w_down[l].astype(BF16), gate2)
    return x
```
